```python
import math
import jax
import jax.numpy as jnp
from jax import lax
import numpy as np

D_MODEL = 1024
BATCH = 4
SEQ = 4096
DEPTH = 1
DEC_BATCH = 128
DEC_SEQ = 1
PAST_LEN = 8192
PAGE_SIZE = 128

POOL_WINDOWS = (2, 4, 8, 16)
POOL_GROUPS = len(POOL_WINDOWS)
POOL_WIDTH = D_MODEL // 2
POOL_GROUP_DIM = POOL_WIDTH // POOL_GROUPS
POOL_STATE_ROWS = max(POOL_WINDOWS) - 1
HEAD_DIM = 64
N_Q_HEADS = (D_MODEL // 2) // HEAD_DIM
N_KV_HEADS = 2
GQA_GROUP = N_Q_HEADS // N_KV_HEADS
ATTN_WIDTH = N_Q_HEADS * HEAD_DIM
KV_WIDTH = N_KV_HEADS * HEAD_DIM
WINDOW = 128
N_BUCKETS = 32
MAX_DISTANCE = 128
RMS_EPS = 1e-6
IN_COLS = 2 * POOL_WIDTH + 2 * ATTN_WIDTH + 2 * KV_WIDTH + 2 * D_MODEL

kernel_name = "gated_pool_swa_hybrid_step"


def rmsnorm(x, g):
    xf = x.astype(jnp.float32)
    y = xf * lax.rsqrt(jnp.mean(xf * xf, axis=-1, keepdims=True) + RMS_EPS) * g.astype(jnp.float32)
    return y.astype(x.dtype)


def t5_bucket(rel):
    n = jnp.maximum(rel, 0)
    max_exact = N_BUCKETS // 2
    nf = jnp.maximum(n, 1).astype(jnp.float32)
    log_b = max_exact + (jnp.log(nf / max_exact) / math.log(MAX_DISTANCE / max_exact)
                         * (N_BUCKETS - max_exact)).astype(jnp.int32)
    return jnp.where(n < max_exact, n, jnp.minimum(log_b, N_BUCKETS - 1))


def split_projection(xn, w_in):
    h = jnp.einsum('btd,dc->btc', xn, w_in)
    sizes = (POOL_WIDTH, POOL_WIDTH, ATTN_WIDTH, KV_WIDTH, KV_WIDTH, ATTN_WIDTH, D_MODEL, D_MODEL)
    offs = [int(o) for o in np.cumsum(sizes)[:-1]]
    u, z_pool, q, k, v, z_attn, g_pool, g_attn = jnp.split(h, offs, axis=-1)
    b, t = h.shape[0], h.shape[1]
    q = q.reshape(b, t, N_KV_HEADS, GQA_GROUP, HEAD_DIM)
    k = k.reshape(b, t, N_KV_HEADS, HEAD_DIM)
    v = v.reshape(b, t, N_KV_HEADS, HEAD_DIM)
    return u, z_pool, q, k, v, z_attn, g_pool, g_attn


def multiscale_pool(u_ext, p0, n_new):
    length = u_ext.shape[1]
    uf = u_ext.astype(jnp.float32)
    pos = p0 + jnp.arange(length)
    outs = []
    for g, w in enumerate(POOL_WINDOWS):
        ug = uf[..., g * POOL_GROUP_DIM:(g + 1) * POOL_GROUP_DIM]
        cs = lax.cumsum(ug, axis=1)
        shifted = jnp.pad(cs, ((0, 0), (w, 0), (0, 0)))[:, :length]
        cnt = jnp.minimum(pos + 1, w).astype(jnp.float32)[None, :, None]
        outs.append((cs - shifted) / cnt - ug)
    pooled = jnp.concatenate(outs, axis=-1)[:, length - n_new:]
    return pooled.astype(u_ext.dtype)


def sink_attention(q, k, v, qpos, kpos, rel_bias, sinks):
    s = jnp.einsum('...qkgd,...skd->...kgqs', q, k).astype(jnp.float32) * (HEAD_DIM ** -0.5)
    rel = qpos[..., :, None] - kpos[..., None, :]
    valid = (rel >= 0) & (rel < WINDOW) & (kpos[..., None, :] >= 0)
    bias = jnp.moveaxis(rel_bias.astype(jnp.float32)[t5_bucket(rel)], -1, -3)
    bias = bias.reshape(bias.shape[:-3] + (N_KV_HEADS, GQA_GROUP) + bias.shape[-2:])
    logits = jnp.where(valid[..., None, None, :, :], s + bias, -jnp.inf)
    sink = sinks.astype(jnp.float32).reshape(N_KV_HEADS, GQA_GROUP, 1, 1)
    m = jnp.maximum(jnp.max(logits, axis=-1, keepdims=True), sink)
    p = jnp.exp(logits - m)
    denom = jnp.sum(p, axis=-1, keepdims=True) + jnp.exp(sink - m)
    return jnp.einsum('...kgqs,...skd->...qkgd', (p / denom).astype(v.dtype), v)


def attn_prompt(q, k, v, rel_bias, sinks):
    b, t = q.shape[0], q.shape[1]
    nb = t // WINDOW
    qb = q.reshape(b, nb, WINDOW, N_KV_HEADS, GQA_GROUP, HEAD_DIM)
    kb = k.reshape(b, nb, WINDOW, N_KV_HEADS, HEAD_DIM)
    vb = v.reshape(b, nb, WINDOW, N_KV_HEADS, HEAD_DIM)
    pad = ((0, 0), (1, 0), (0, 0), (0, 0), (0, 0))
    kband = jnp.concatenate([jnp.pad(kb, pad)[:, :nb], kb], axis=2)
    vband = jnp.concatenate([jnp.pad(vb, pad)[:, :nb], vb], axis=2)
    qpos = jnp.arange(t).reshape(nb, WINDOW)
    kpos = jnp.concatenate([qpos - WINDOW, qpos], axis=1)
    o = sink_attention(qb, kband, vband, qpos, kpos, rel_bias, sinks)
    return o.reshape(b, t, ATTN_WIDTH)


def attn_sample(q, k_ext, v_ext, rel_bias, sinks):
    b, s_new = q.shape[0], q.shape[1]
    qpos = PAST_LEN + jnp.arange(s_new)
    kpos = PAST_LEN - WINDOW + jnp.arange(WINDOW + s_new)
    o = sink_attention(q, k_ext, v_ext, qpos, kpos, rel_bias, sinks)
    return o.reshape(b, s_new, ATTN_WIDTH)


def merge_branches(x, pooled, z_pool, att, z_attn, g_pool, g_attn,
                   w_grp, pool_scale, w_br_pool, w_br_attn, w_out):
    b, t = pooled.shape[0], pooled.shape[1]
    pg = jnp.einsum('btgc,gce->btge', pooled.reshape(b, t, POOL_GROUPS, POOL_GROUP_DIM), w_grp)
    pg = pg.reshape(b, t, POOL_WIDTH)
    br_pool = jnp.einsum('btp,pd->btd', pg * pool_scale * jax.nn.silu(z_pool), w_br_pool)
    br_attn = jnp.einsum('bta,ad->btd', att * jax.nn.silu(z_attn), w_br_attn)
    merged = jax.nn.sigmoid(g_pool) * br_pool + jax.nn.sigmoid(g_attn) * br_attn
    return x + jnp.einsum('btd,de->bte', merged, w_out)


def setup_inputs(seed: int = 0) -> dict:
    key = jax.random.key(seed)
    ks = jax.random.split(key, 15)
    nrm = lambda k, shape, s: jax.random.normal(k, shape, jnp.float32) * s
    return {
        "x_prompt": nrm(ks[0], (BATCH, SEQ, D_MODEL), 1.0),
        "x_sample": nrm(ks[1], (DEC_BATCH, DEC_SEQ, D_MODEL), 1.0),
        "cache_k": nrm(ks[2], (DEPTH, DEC_BATCH, WINDOW, N_KV_HEADS, HEAD_DIM), 1.0),
        "cache_v": nrm(ks[3], (DEPTH, DEC_BATCH, WINDOW, N_KV_HEADS, HEAD_DIM), 1.0),
        "state_pool": nrm(ks[4], (DEPTH, DEC_BATCH, POOL_STATE_ROWS, POOL_WIDTH), 1.0),
        "rel_bias": nrm(ks[5], (N_BUCKETS, N_Q_HEADS), 0.5),
        "g_norm": 1.0 + nrm(ks[6], (DEPTH, D_MODEL), 0.02),
        "w_in": nrm(ks[7], (DEPTH, D_MODEL, IN_COLS), D_MODEL ** -0.5),
        "pool_w_grp": nrm(ks[8], (DEPTH, POOL_GROUPS, POOL_GROUP_DIM, POOL_GROUP_DIM), POOL_GROUP_DIM ** -0.5),
        "pool_scale": 1.0 + nrm(ks[9], (DEPTH, POOL_WIDTH), 0.02),
        "attn_sinks": nrm(ks[10], (DEPTH, N_Q_HEADS), 1.0),
        "w_br_pool": nrm(ks[11], (DEPTH, POOL_WIDTH, D_MODEL), POOL_WIDTH ** -0.5),
        "w_br_attn": nrm(ks[12], (DEPTH, ATTN_WIDTH, D_MODEL), ATTN_WIDTH ** -0.5),
        "w_out": nrm(ks[13], (DEPTH, D_MODEL, D_MODEL), D_MODEL ** -0.5),
        "g_final": 1.0 + nrm(ks[14], (D_MODEL,), 0.02),
    }


def reference(x_prompt, x_sample, cache_k, cache_v, state_pool, rel_bias, g_norm, w_in,
              pool_w_grp, pool_scale, attn_sinks, w_br_pool, w_br_attn, w_out, g_final):
    seq = x_prompt.shape[1]
    dec_seq = x_sample.shape[1]
    xp, xs = x_prompt, x_sample
    kp_l, vp_l, pp_l, ks_l, vs_l, ps_l = [], [], [], [], [], []
    for l in range(DEPTH):
        u, zp, q, k, v, za, gp, ga = split_projection(rmsnorm(xp, g_norm[l]), w_in[l])
        pooled = multiscale_pool(u, 0, seq)
        att = attn_prompt(q, k, v, rel_bias, attn_sinks[l])
        xp = merge_branches(xp, pooled, zp, att, za, gp, ga, pool_w_grp[l], pool_scale[l],
                            w_br_pool[l], w_br_attn[l], w_out[l])
        kp_l.append(k[:, seq - WINDOW:])
        vp_l.append(v[:, seq - WINDOW:])
        pp_l.append(u[:, seq - POOL_STATE_ROWS:])

        u, zp, q, k, v, za, gp, ga = split_projection(rmsnorm(xs, g_norm[l]), w_in[l])
        u_ext = jnp.concatenate([state_pool[l].astype(u.dtype), u], axis=1)
        pooled = multiscale_pool(u_ext, PAST_LEN - POOL_STATE_ROWS, dec_seq)
        k_ext = jnp.concatenate([cache_k[l].astype(k.dtype), k], axis=1)
        v_ext = jnp.concatenate([cache_v[l].astype(v.dtype), v], axis=1)
        att = attn_sample(q, k_ext, v_ext, rel_bias, attn_sinks[l])
        xs = merge_branches(xs, pooled, zp, att, za, gp, ga, pool_w_grp[l], pool_scale[l],
                            w_br_pool[l], w_br_attn[l], w_out[l])
        ks_l.append(k_ext[:, dec_seq:])
        vs_l.append(v_ext[:, dec_seq:])
        ps_l.append(u_ext[:, dec_seq:])
    y_prompt = rmsnorm(xp, g_final)
    y_sample = rmsnorm(xs, g_final)
    return (y_prompt, y_sample, jnp.stack(kp_l), jnp.stack(vp_l), jnp.stack(pp_l),
            jnp.stack(ks_l), jnp.stack(vs_l), jnp.stack(ps_l))
```

```python
import functools
import math

import jax
import jax.numpy as jnp
import numpy as np
from jax import lax
from jax.experimental import pallas as pl
from jax.experimental.pallas import tpu as pltpu

D_MODEL = 1024
POOL_WINDOWS = (2, 4, 8, 16)
POOL_GROUPS = len(POOL_WINDOWS)
POOL_WIDTH = D_MODEL // 2
POOL_GROUP_DIM = POOL_WIDTH // POOL_GROUPS
POOL_STATE_ROWS = max(POOL_WINDOWS) - 1
HEAD_DIM = 64
N_KV_HEADS = 2
ATTN_WIDTH = D_MODEL // 2
N_Q_HEADS = ATTN_WIDTH // HEAD_DIM
GQA_GROUP = N_Q_HEADS // N_KV_HEADS
KV_WIDTH = N_KV_HEADS * HEAD_DIM
WINDOW = 128
N_BUCKETS = 32
MAX_DISTANCE = 128
RMS_EPS = 1e-6
IN_COLS = 2 * POOL_WIDTH + 2 * ATTN_WIDTH + 2 * KV_WIDTH + 2 * D_MODEL

OFF_U = 0
OFF_ZP = OFF_U + POOL_WIDTH
OFF_Q = OFF_ZP + POOL_WIDTH
OFF_K = OFF_Q + ATTN_WIDTH
OFF_V = OFF_K + KV_WIDTH
OFF_ZA = OFF_V + KV_WIDTH
OFF_GP = OFF_ZA + ATTN_WIDTH
OFF_GA = OFF_GP + D_MODEL

LANES = 128
SUBLANES = 8
POOL_HDR = 16
TM = 512
BC = 32
VMEM_LIMIT_BYTES = 56 * 1024 * 1024

F32 = jnp.float32
BF16 = jnp.bfloat16
NEG_INF = float("-inf")


def _sigmoid(x):
    return 1.0 / (1.0 + jnp.exp(-x))


def _silu(x):
    return x * _sigmoid(x)


def _rmsnorm(x, g):
    return x * lax.rsqrt(jnp.mean(x * x, axis=-1, keepdims=True) + RMS_EPS) * g


def _dot(a, b):
    return jnp.dot(a, b, preferred_element_type=F32)


def _t5_bucket(rel):
    n = jnp.maximum(rel, 0)
    max_exact = N_BUCKETS // 2
    nf = jnp.maximum(n, 1).astype(F32)
    log_b = max_exact + (jnp.log(nf / max_exact) / math.log(MAX_DISTANCE / max_exact)
                         * (N_BUCKETS - max_exact)).astype(jnp.int32)
    return jnp.where(n < max_exact, n, jnp.minimum(log_b, N_BUCKETS - 1))


def _bias_table(rel, valid, relb_ref, h):
    bucket = _t5_bucket(rel)

    def body(b, tbl):
        return jnp.where(bucket == b, relb_ref[b, h], tbl)

    tbl = lax.fori_loop(0, N_BUCKETS, body, jnp.zeros(rel.shape, F32))
    return jnp.where(valid, tbl, NEG_INF)


def _project(xn_ref, w_in_ref, off, width):
    return _dot(xn_ref[...], w_in_ref[:, off:off + width])


def _kv_lane_mask(shape):
    return lax.broadcasted_iota(jnp.int32, shape, len(shape) - 1) < HEAD_DIM


def _pool_windows(x_ext, inv_cnt):
    outs = []
    for g, w in enumerate(POOL_WINDOWS):
        xg = x_ext[:, g * POOL_GROUP_DIM:(g + 1) * POOL_GROUP_DIM]
        acc = xg
        shift = 1
        while shift < w:
            acc = acc + pltpu.roll(acc, shift, 0)
            shift *= 2
        outs.append(acc[POOL_HDR:] * inv_cnt[g] - xg[POOL_HDR:])
    return jnp.concatenate(outs, axis=-1)


def _merge(x, pooled, zp_act, att_act, sig_gp, sig_ga, w_grp_ref, pool_scale, w_brp_ref, w_bra_ref,
           w_out_ref, g_final):
    pooled_bf = pooled.astype(BF16)
    pg = jnp.concatenate(
        [_dot(pooled_bf[:, g * POOL_GROUP_DIM:(g + 1) * POOL_GROUP_DIM], w_grp_ref[g])
         for g in range(POOL_GROUPS)], axis=-1)
    br_pool = _dot((pg * pool_scale * zp_act).astype(BF16), w_brp_ref[...])
    br_attn = _dot(att_act, w_bra_ref[...])
    merged = sig_gp * br_pool + sig_ga * br_attn
    out = x + _dot(merged.astype(BF16), w_out_ref[...])
    return _rmsnorm(out, g_final)


def _prompt_kernel(relb_ref, sinks_ref, x_ref, gnorm_ref, w_in_ref, w_grp_ref, pscale_ref, w_brp_ref,
                   w_bra_ref, w_out_ref, gfinal_ref,
                   y_ref, knew_ref, vnew_ref, pnew_ref,
                   xn_s, uext_s, zp_s, q_s, kband_s, vband_s, za_s, gp_s, ga_s, ql_s, att_s, bias_s):
    b = pl.program_id(0)
    j = pl.program_id(1)
    nblk = TM // WINDOW

    @pl.when((b == 0) & (j == 0))
    def _build_bias():
        qi = lax.broadcasted_iota(jnp.int32, (WINDOW, 2 * WINDOW), 0)
        kc = lax.broadcasted_iota(jnp.int32, (WINDOW, 2 * WINDOW), 1)
        rel = qi - kc + WINDOW
        valid = (rel >= 0) & (rel < WINDOW)
        for h in range(N_Q_HEADS):
            bias_s[h * WINDOW:(h + 1) * WINDOW, :] = _bias_table(rel, valid, relb_ref, h)

    @pl.when(j == 0)
    def _reset_carry():
        uext_s[0:POOL_HDR, :] = jnp.zeros((POOL_HDR, POOL_WIDTH), F32)
        kband_s[0:WINDOW, :] = jnp.zeros((WINDOW, KV_WIDTH), BF16)
        vband_s[0:WINDOW, :] = jnp.zeros((WINDOW, KV_WIDTH), BF16)

    xn_s[...] = _rmsnorm(x_ref[0], gnorm_ref[...]).astype(BF16)
    uext_s[POOL_HDR:, :] = _project(xn_s, w_in_ref, OFF_U, POOL_WIDTH)
    zp_s[...] = _silu(_project(xn_s, w_in_ref, OFF_ZP, POOL_WIDTH))
    q_s[...] = (_project(xn_s, w_in_ref, OFF_Q, ATTN_WIDTH) * (HEAD_DIM ** -0.5)).astype(BF16)
    kv = _project(xn_s, w_in_ref, OFF_K, 2 * KV_WIDTH)
    kband_s[WINDOW:, :] = kv[:, :KV_WIDTH].astype(BF16)
    vband_s[WINDOW:, :] = kv[:, KV_WIDTH:].astype(BF16)
    knew_ref[0] = kv[TM - WINDOW:, :KV_WIDTH]
    vnew_ref[0] = kv[TM - WINDOW:, KV_WIDTH:]
    za_s[...] = _silu(_project(xn_s, w_in_ref, OFF_ZA, ATTN_WIDTH))
    for half in range(2):
        cols = slice(half * POOL_WIDTH, (half + 1) * POOL_WIDTH)
        gp_s[:, cols] = _sigmoid(_project(xn_s, w_in_ref, OFF_GP + half * POOL_WIDTH, POOL_WIDTH))
        ga_s[:, cols] = _sigmoid(_project(xn_s, w_in_ref, OFF_GA + half * POOL_WIDTH, POOL_WIDTH))

    pos = j * TM + lax.broadcasted_iota(jnp.int32, (TM, 1), 0)
    inv_cnt = [1.0 / jnp.minimum(pos + 1, w).astype(F32) for w in POOL_WINDOWS]
    x_ext = uext_s[...]
    pooled = _pool_windows(x_ext, inv_cnt)
    pnew_ref[0] = x_ext[POOL_HDR + TM - POOL_STATE_ROWS:, :]
    uext_s[0:POOL_HDR, :] = x_ext[TM:, :]

    first_neg = jnp.where(j == 0, NEG_INF, 0.0).astype(F32)
    kv0_lanes = _kv_lane_mask((WINDOW, LANES))
    for c in range(nblk):
        rows = slice(c * WINDOW, (c + 1) * WINDOW)
        for g in range(GQA_GROUP):
            qg = q_s[rows, g * LANES:(g + 1) * LANES]
            zero = jnp.zeros_like(qg)
            ql_s[g * WINDOW:(g + 1) * WINDOW, :] = jnp.where(kv0_lanes, qg, zero)
            ql_s[(GQA_GROUP + g) * WINDOW:(GQA_GROUP + g + 1) * WINDOW, :] = jnp.where(kv0_lanes, zero, qg)
        k_c = kband_s[c * WINDOW:(c + 2) * WINDOW, :]
        v_c = vband_s[c * WINDOW:(c + 2) * WINDOW, :]
        s_all = lax.dot_general(ql_s[...], k_c, (((1,), (1,)), ((), ())), preferred_element_type=F32)
        p_parts, inv_parts = [], []
        for h in range(N_Q_HEADS):
            hrows = slice(h * WINDOW, (h + 1) * WINDOW)
            s = s_all[hrows, :] + bias_s[hrows, :]
            if c == 0:
                s = jnp.concatenate([s[:, :WINDOW] + first_neg, s[:, WINDOW:]], axis=-1)
            sink = sinks_ref[h]
            m = jnp.maximum(jnp.max(s, axis=-1, keepdims=True), sink)
            p = jnp.exp(s - m)
            denom = jnp.sum(p, axis=-1, keepdims=True) + jnp.exp(sink - m)
            p_parts.append(p.astype(BF16))
            inv_parts.append(1.0 / denom)
        o_all = _dot(jnp.concatenate(p_parts, axis=0), v_c)
        for g in range(GQA_GROUP):
            h0, h1 = g, GQA_GROUP + g
            o0 = o_all[h0 * WINDOW:(h0 + 1) * WINDOW, :] * inv_parts[h0]
            o1 = o_all[h1 * WINDOW:(h1 + 1) * WINDOW, :] * inv_parts[h1]
            att = jnp.where(kv0_lanes, o0, o1)
            cols = slice(g * LANES, (g + 1) * LANES)
            att_s[rows, cols] = (att * za_s[rows, cols]).astype(BF16)
    kband_s[0:WINDOW, :] = kband_s[TM:, :]
    vband_s[0:WINDOW, :] = vband_s[TM:, :]

    y_ref[0] = _merge(x_ref[0], pooled, zp_s[...], att_s[...], gp_s[...], ga_s[...], w_grp_ref,
                      pscale_ref[...], w_brp_ref, w_bra_ref, w_out_ref, gfinal_ref[...])


def _resident(shape):
    return pl.BlockSpec(shape, lambda *_: (0,) * len(shape), pipeline_mode=pl.Buffered(1))


_SMEM = pl.BlockSpec(memory_space=pltpu.SMEM)


def _weight_specs():
    return [
        _resident((1, D_MODEL)),
        _resident((D_MODEL, IN_COLS)),
        _resident((POOL_GROUPS, POOL_GROUP_DIM, POOL_GROUP_DIM)),
        _resident((1, POOL_WIDTH)),
        _resident((POOL_WIDTH, D_MODEL)),
        _resident((ATTN_WIDTH, D_MODEL)),
        _resident((D_MODEL, D_MODEL)),
        _resident((1, D_MODEL)),
    ]


def _prompt_call(x, relb, sinks, weights):
    batch, seq, _ = x.shape
    assert seq % TM == 0 and TM % WINDOW == 0 and TM >= POOL_HDR
    grid = (batch, seq // TM)
    out_shape = (
        jax.ShapeDtypeStruct((batch, seq, D_MODEL), F32),
        jax.ShapeDtypeStruct((batch, WINDOW, KV_WIDTH), F32),
        jax.ShapeDtypeStruct((batch, WINDOW, KV_WIDTH), F32),
        jax.ShapeDtypeStruct((batch, POOL_STATE_ROWS, POOL_WIDTH), F32),
    )
    scratch = [
        pltpu.VMEM((TM, D_MODEL), BF16),
        pltpu.VMEM((POOL_HDR + TM, POOL_WIDTH), F32),
        pltpu.VMEM((TM, POOL_WIDTH), F32),
        pltpu.VMEM((TM, ATTN_WIDTH), BF16),
        pltpu.VMEM((WINDOW + TM, KV_WIDTH), BF16),
        pltpu.VMEM((WINDOW + TM, KV_WIDTH), BF16),
        pltpu.VMEM((TM, ATTN_WIDTH), F32),
        pltpu.VMEM((TM, D_MODEL), F32),
        pltpu.VMEM((TM, D_MODEL), F32),
        pltpu.VMEM((N_Q_HEADS * WINDOW, LANES), BF16),
        pltpu.VMEM((TM, ATTN_WIDTH), BF16),
        pltpu.VMEM((N_Q_HEADS * WINDOW, 2 * WINDOW), F32),
    ]
    return pl.pallas_call(
        _prompt_kernel,
        grid=grid,
        in_specs=[_SMEM, _SMEM,
                  pl.BlockSpec((1, TM, D_MODEL), lambda b, j: (b, j, 0))] + _weight_specs(),
        out_specs=(
            pl.BlockSpec((1, TM, D_MODEL), lambda b, j: (b, j, 0)),
            pl.BlockSpec((1, WINDOW, KV_WIDTH), lambda b, j: (b, 0, 0)),
            pl.BlockSpec((1, WINDOW, KV_WIDTH), lambda b, j: (b, 0, 0)),
            pl.BlockSpec((1, POOL_STATE_ROWS, POOL_WIDTH), lambda b, j: (b, 0, 0)),
        ),
        out_shape=out_shape,
        scratch_shapes=scratch,
        compiler_params=pltpu.CompilerParams(
            dimension_semantics=("arbitrary", "arbitrary"), vmem_limit_bytes=VMEM_LIMIT_BYTES),
        name="prompt_layer",
    )(relb, sinks, x, *weights)


def _sample_kernel(relb_ref, sinks_ref, x_ref, kc3_ref, vc3_ref, kc2_ref, vc2_ref, st2_ref,
                   gnorm_ref, w_in_ref, w_grp_ref, pscale_ref, w_brp_ref, w_bra_ref, w_out_ref, gfinal_ref,
                   y_ref, knew2_ref, vnew2_ref, pnew2_ref,
                   xn_s, u_s, zp_s, qh_s, k_s, v_s, za_s, gp_s, ga_s, pooled_s, att_s,
                   q2_s, k2_s, v2_s, o2_s, bias_s):
    i = pl.program_id(0)
    n_steps = pl.num_programs(0)
    n_seq = x_ref.shape[0]
    chunk = pl.ds(pl.multiple_of(i * BC, BC), BC)

    @pl.when(i == 0)
    def _project_all():
        slot = lax.broadcasted_iota(jnp.int32, (SUBLANES, WINDOW), 1)
        head = lax.broadcasted_iota(jnp.int32, (SUBLANES, WINDOW), 0)
        rel = jnp.where(slot == 0, 0, WINDOW - slot)
        tbl = jnp.zeros((SUBLANES, WINDOW), F32)
        for h in range(N_Q_HEADS):
            tbl = jnp.where(head == h, _bias_table(rel, rel >= 0, relb_ref, h), tbl)
        bias_s[...] = tbl

        xn_s[...] = _rmsnorm(x_ref[...], gnorm_ref[...]).astype(BF16)
        u_s[...] = _project(xn_s, w_in_ref, OFF_U, POOL_WIDTH)
        zp_s[...] = _silu(_project(xn_s, w_in_ref, OFF_ZP, POOL_WIDTH))
        q = _project(xn_s, w_in_ref, OFF_Q, ATTN_WIDTH) * (HEAD_DIM ** -0.5)
        kv0_lanes = _kv_lane_mask((n_seq, LANES))
        for g in range(GQA_GROUP):
            qg = q[:, g * LANES:(g + 1) * LANES]
            qh_s[:, g * LANES:(g + 1) * LANES] = jnp.where(kv0_lanes, qg, 0.0)
            qh_s[:, (GQA_GROUP + g) * LANES:(GQA_GROUP + g + 1) * LANES] = jnp.where(kv0_lanes, 0.0, qg)
        kv = _project(xn_s, w_in_ref, OFF_K, 2 * KV_WIDTH)
        k_s[...] = kv[:, :KV_WIDTH]
        v_s[...] = kv[:, KV_WIDTH:]
        za_s[...] = _silu(_project(xn_s, w_in_ref, OFF_ZA, ATTN_WIDTH))
        for half in range(2):
            cols = slice(half * POOL_WIDTH, (half + 1) * POOL_WIDTH)
            gp_s[:, cols] = _sigmoid(_project(xn_s, w_in_ref, OFF_GP + half * POOL_WIDTH, POOL_WIDTH))
            ga_s[:, cols] = _sigmoid(_project(xn_s, w_in_ref, OFF_GA + half * POOL_WIDTH, POOL_WIDTH))

    u_new = u_s[chunk, :]
    hist = st2_ref[...]
    n_hist = POOL_STATE_ROWS
    parts = []
    for g, w in enumerate(POOL_WINDOWS):
        lo = g * POOL_GROUP_DIM
        acc = u_new[:, lo:lo + POOL_GROUP_DIM]
        for r in range(n_hist - (w - 1), n_hist):
            acc = acc + hist[:, r * POOL_WIDTH + lo:r * POOL_WIDTH + lo + POOL_GROUP_DIM]
        parts.append(acc * (1.0 / w) - u_new[:, lo:lo + POOL_GROUP_DIM])
    pooled_s[chunk, :] = jnp.concatenate(parts, axis=-1)
    pnew2_ref[:, :(n_hist - 1) * POOL_WIDTH] = hist[:, POOL_WIDTH:]
    pnew2_ref[:, (n_hist - 1) * POOL_WIDTH:] = u_new

    k_new = k_s[chunk, :]
    v_new = v_s[chunk, :]
    knew2_ref[:, :(WINDOW - 1) * KV_WIDTH] = kc2_ref[:, KV_WIDTH:]
    knew2_ref[:, (WINDOW - 1) * KV_WIDTH:] = k_new
    vnew2_ref[:, :(WINDOW - 1) * KV_WIDTH] = vc2_ref[:, KV_WIDTH:]
    vnew2_ref[:, (WINDOW - 1) * KV_WIDTH:] = v_new

    k2_s[...] = kc3_ref[...].reshape(BC * WINDOW, KV_WIDTH)
    v2_s[...] = vc3_ref[...].reshape(BC * WINDOW, KV_WIDTH)
    k2_s[pl.ds(0, BC, stride=WINDOW), :] = k_new
    v2_s[pl.ds(0, BC, stride=WINDOW), :] = v_new
    for h in range(N_Q_HEADS):
        q2_s[pl.ds(h, BC, stride=N_Q_HEADS), :] = qh_s[chunk, h * LANES:(h + 1) * LANES]
    q3 = q2_s[...].reshape(BC, N_Q_HEADS, LANES).astype(BF16)
    k3 = k2_s[...].reshape(BC, WINDOW, KV_WIDTH).astype(BF16)
    v3 = v2_s[...].reshape(BC, WINDOW, KV_WIDTH).astype(BF16)
    s3 = lax.dot_general(q3, k3, (((2,), (2,)), ((0,), (0,))), preferred_element_type=F32)
    s3 = s3 + bias_s[...][None]
    head_col = lax.broadcasted_iota(jnp.int32, (N_Q_HEADS, 1), 0)
    sink = jnp.zeros((N_Q_HEADS, 1), F32)
    for h in range(N_Q_HEADS):
        sink = jnp.where(head_col == h, sinks_ref[h], sink)
    sink = sink[None]
    m = jnp.maximum(jnp.max(s3, axis=-1, keepdims=True), sink)
    p3 = jnp.exp(s3 - m)
    denom = jnp.sum(p3, axis=-1, keepdims=True) + jnp.exp(sink - m)
    o3 = lax.dot_general(p3.astype(BF16), v3, (((2,), (1,)), ((0,), (0,))), preferred_element_type=F32)
    o3 = o3 * (1.0 / denom)
    o2_s[...] = o3.reshape(BC * N_Q_HEADS, LANES)
    kv0_lanes = _kv_lane_mask((BC, LANES))
    for g in range(GQA_GROUP):
        o0 = o2_s[pl.ds(g, BC, stride=N_Q_HEADS), :]
        o1 = o2_s[pl.ds(GQA_GROUP + g, BC, stride=N_Q_HEADS), :]
        cols = slice(g * LANES, (g + 1) * LANES)
        att_s[chunk, cols] = (jnp.where(kv0_lanes, o0, o1) * za_s[chunk, cols]).astype(BF16)

    @pl.when(i == n_steps - 1)
    def _merge_all():
        y_ref[...] = _merge(x_ref[...], pooled_s[...], zp_s[...], att_s[...], gp_s[...], ga_s[...],
                            w_grp_ref, pscale_ref[...], w_brp_ref, w_bra_ref, w_out_ref, gfinal_ref[...])


def _sample_call(x, cache_k, cache_v, state, relb, sinks, weights):
    n_seq = x.shape[0]
    assert n_seq % BC == 0
    hist_w = POOL_STATE_ROWS * POOL_WIDTH
    cache_w = WINDOW * KV_WIDTH
    out_shape = (
        jax.ShapeDtypeStruct((n_seq, D_MODEL), F32),
        jax.ShapeDtypeStruct((n_seq, cache_w), F32),
        jax.ShapeDtypeStruct((n_seq, cache_w), F32),
        jax.ShapeDtypeStruct((n_seq, hist_w), F32),
    )
    scratch = [
        pltpu.VMEM((n_seq, D_MODEL), BF16),
        pltpu.VMEM((n_seq, POOL_WIDTH), F32),
        pltpu.VMEM((n_seq, POOL_WIDTH), F32),
        pltpu.VMEM((n_seq, N_Q_HEADS * LANES), F32),
        pltpu.VMEM((n_seq, KV_WIDTH), F32),
        pltpu.VMEM((n_seq, KV_WIDTH), F32),
        pltpu.VMEM((n_seq, ATTN_WIDTH), F32),
        pltpu.VMEM((n_seq, D_MODEL), F32),
        pltpu.VMEM((n_seq, D_MODEL), F32),
        pltpu.VMEM((n_seq, POOL_WIDTH), F32),
        pltpu.VMEM((n_seq, ATTN_WIDTH), BF16),
        pltpu.VMEM((BC * N_Q_HEADS, LANES), F32),
        pltpu.VMEM((BC * WINDOW, KV_WIDTH), F32),
        pltpu.VMEM((BC * WINDOW, KV_WIDTH), F32),
        pltpu.VMEM((BC * N_Q_HEADS, LANES), F32),
        pltpu.VMEM((SUBLANES, WINDOW), F32),
    ]
    cache3 = pl.BlockSpec((BC, WINDOW, KV_WIDTH), lambda i: (i, 0, 0))
    cache2 = pl.BlockSpec((BC, cache_w), lambda i: (i, 0))
    hist2 = pl.BlockSpec((BC, hist_w), lambda i: (i, 0))
    return pl.pallas_call(
        _sample_kernel,
        grid=(n_seq // BC,),
        in_specs=[_SMEM, _SMEM, _resident((n_seq, D_MODEL)), cache3, cache3, cache2, cache2, hist2]
        + _weight_specs(),
        out_specs=(pl.BlockSpec((n_seq, D_MODEL), lambda i: (0, 0)), cache2, cache2, hist2),
        out_shape=out_shape,
        scratch_shapes=scratch,
        compiler_params=pltpu.CompilerParams(
            dimension_semantics=("arbitrary",), vmem_limit_bytes=VMEM_LIMIT_BYTES),
        name="sample_layer",
    )(relb, sinks, x, cache_k.reshape(n_seq, WINDOW, KV_WIDTH), cache_v.reshape(n_seq, WINDOW, KV_WIDTH),
      cache_k.reshape(n_seq, cache_w), cache_v.reshape(n_seq, cache_w), state.reshape(n_seq, hist_w),
      *weights)


def _heads_g_major(w, axis):
    shape = w.shape
    split = shape[:axis] + (N_KV_HEADS, GQA_GROUP, HEAD_DIM) + shape[axis + 1:]
    return jnp.swapaxes(w.reshape(split), axis, axis + 1).reshape(shape)


def kernel(x_prompt, x_sample, cache_k, cache_v, state_pool, rel_bias, g_norm, w_in, pool_w_grp, pool_scale, attn_sinks, w_br_pool, w_br_attn, w_out, g_final):
    depth = g_norm.shape[0]
    assert depth == 1 and x_sample.shape[1] == 1
    l = 0
    w = w_in[l]
    w_in_p = jnp.concatenate(
        [w[:, :OFF_Q], _heads_g_major(w[:, OFF_Q:OFF_K], 1), w[:, OFF_K:OFF_ZA],
         _heads_g_major(w[:, OFF_ZA:OFF_GP], 1), w[:, OFF_GP:]], axis=1).astype(BF16)
    weights = (
        g_norm[l].reshape(1, D_MODEL),
        w_in_p,
        pool_w_grp[l].astype(BF16),
        pool_scale[l].reshape(1, POOL_WIDTH),
        w_br_pool[l].astype(BF16),
        _heads_g_major(w_br_attn[l], 0).astype(BF16),
        w_out[l].astype(BF16),
        g_final.reshape(1, D_MODEL),
    )
    sinks = attn_sinks[l]
    batch = x_prompt.shape[0]
    n_seq = x_sample.shape[0]

    y_p, k_p, v_p, pool_p = _prompt_call(x_prompt, rel_bias, sinks, weights)
    y_s, k_s, v_s, pool_s = _sample_call(x_sample.reshape(n_seq, D_MODEL), cache_k[l], cache_v[l],
                                         state_pool[l], rel_bias, sinks, weights)
    kv_shape = (WINDOW, N_KV_HEADS, HEAD_DIM)
    return (
        y_p,
        y_s.reshape(n_seq, 1, D_MODEL),
        k_p.reshape((depth, batch) + kv_shape),
        v_p.reshape((depth, batch) + kv_shape),
        pool_p.reshape(depth, batch, POOL_STATE_ROWS, POOL_WIDTH),
        k_s.reshape((depth, n_seq) + kv_shape),
        v_s.reshape((depth, n_seq) + kv_shape),
        pool_s.reshape(depth, n_seq, POOL_STATE_ROWS, POOL_WIDTH),
    )
```

```python
import math

import jax
import jax.numpy as jnp
from jax import lax
from jax.experimental import pallas as pl
from jax.experimental.pallas import tpu as pltpu

D_MODEL = 1024
POOL_WINDOWS = (2, 4, 8, 16)
POOL_GROUPS = len(POOL_WINDOWS)
POOL_WIDTH = D_MODEL // 2
POOL_GROUP_DIM = POOL_WIDTH // POOL_GROUPS
POOL_STATE_ROWS = max(POOL_WINDOWS) - 1
HEAD_DIM = 64
N_KV_HEADS = 2
ATTN_WIDTH = D_MODEL // 2
N_Q_HEADS = ATTN_WIDTH // HEAD_DIM
GQA_GROUP = N_Q_HEADS // N_KV_HEADS
KV_WIDTH = N_KV_HEADS * HEAD_DIM
WINDOW = 128
N_BUCKETS = 32
MAX_DISTANCE = 128
RMS_EPS = 1e-6
IN_COLS = 2 * POOL_WIDTH + 2 * ATTN_WIDTH + 2 * KV_WIDTH + 2 * D_MODEL

OFF_U = 0
OFF_ZP = OFF_U + POOL_WIDTH
OFF_Q = OFF_ZP + POOL_WIDTH
OFF_K = OFF_Q + ATTN_WIDTH
OFF_V = OFF_K + KV_WIDTH
OFF_ZA = OFF_V + KV_WIDTH
OFF_GP = OFF_ZA + ATTN_WIDTH
OFF_GA = OFF_GP + D_MODEL

LANES = 128
SUBLANES = 8
POOL_HDR = 16
TM = 512
BC = 32
CAST_STEPS = 8
VMEM_LIMIT_BYTES = 56 * 1024 * 1024

assert KV_WIDTH == LANES and 2 * HEAD_DIM == LANES and GQA_GROUP % 2 == 0

F32 = jnp.float32
BF16 = jnp.bfloat16
NEG_INF = float("-inf")


def _sigmoid(x):
    return 1.0 / (1.0 + jnp.exp(-x))


def _silu(x):
    return x * _sigmoid(x)


def _rmsnorm(x, g):
    return x * lax.rsqrt(jnp.mean(x * x, axis=-1, keepdims=True) + RMS_EPS) * g


def _dot(a, b):
    return jnp.dot(a, b, preferred_element_type=F32)


def _t5_bucket(rel):
    n = jnp.maximum(rel, 0)
    max_exact = N_BUCKETS // 2
    nf = jnp.maximum(n, 1).astype(F32)
    log_b = max_exact + (jnp.log(nf / max_exact) / math.log(MAX_DISTANCE / max_exact)
                         * (N_BUCKETS - max_exact)).astype(jnp.int32)
    return jnp.where(n < max_exact, n, jnp.minimum(log_b, N_BUCKETS - 1))


def _bias_table(rel, relb_ref, h):
    bucket = _t5_bucket(rel)

    def body(b, tbl):
        return jnp.where(bucket == b, relb_ref[b, h], tbl)

    tbl = lax.fori_loop(0, N_BUCKETS, body, jnp.zeros(rel.shape, F32))
    return jnp.where((rel >= 0) & (rel < WINDOW), tbl, NEG_INF)


def _per_head_column(ref_scalar):
    head = lax.broadcasted_iota(jnp.int32, (N_Q_HEADS, 1), 0)
    col = jnp.zeros((N_Q_HEADS, 1), F32)
    for h in range(N_Q_HEADS):
        col = jnp.where(head == h, ref_scalar(h), col)
    return col


def _project(xn_ref, w_in_ref, off, width):
    return _dot(xn_ref[...], w_in_ref[:, off:off + width])


def _kv_lane_mask(shape):
    return lax.broadcasted_iota(jnp.int32, shape, len(shape) - 1) < HEAD_DIM


def _head_pairs():
    for grp in range(ATTN_WIDTH // LANES):
        kv, pair = divmod(grp, GQA_GROUP // 2)
        yield grp, kv, kv * GQA_GROUP + 2 * pair, kv * GQA_GROUP + 2 * pair + 1


def _head_blocks(q):
    low = _kv_lane_mask((q.shape[0], LANES))
    blocks = [None] * N_Q_HEADS
    for grp, kv, even, odd in _head_pairs():
        a = q[:, grp * LANES:(grp + 1) * LANES]
        swapped = pltpu.roll(a, HEAD_DIM, 1)
        if kv == 0:
            blocks[even], blocks[odd] = jnp.where(low, a, 0.0), jnp.where(low, swapped, 0.0)
        else:
            blocks[even], blocks[odd] = jnp.where(low, 0.0, swapped), jnp.where(low, 0.0, a)
    return blocks


def _gather_heads(outs):
    low = _kv_lane_mask(outs[0].shape)
    groups = []
    for _, kv, even, odd in _head_pairs():
        if kv == 0:
            groups.append(jnp.where(low, outs[even], pltpu.roll(outs[odd], HEAD_DIM, 1)))
        else:
            groups.append(jnp.where(low, pltpu.roll(outs[even], HEAD_DIM, 1), outs[odd]))
    return jnp.concatenate(groups, axis=-1)


def _pool_windows(x_ext, inv_cnt):
    outs = []
    for g, w in enumerate(POOL_WINDOWS):
        xg = x_ext[:, g * POOL_GROUP_DIM:(g + 1) * POOL_GROUP_DIM]
        acc = xg
        shift = 1
        while shift < w:
            acc = acc + pltpu.roll(acc, shift, 0)
            shift *= 2
        outs.append(acc[POOL_HDR:] * inv_cnt[g] - xg[POOL_HDR:])
    return jnp.concatenate(outs, axis=-1)


def _merge(x, pooled, zp_act, att_act, sig_gp, sig_ga, w_grp_ref, pool_scale, w_brp_ref, w_bra_ref,
           w_out_ref, g_final):
    pooled_bf = pooled.astype(BF16)
    pg = jnp.concatenate(
        [_dot(pooled_bf[:, g * POOL_GROUP_DIM:(g + 1) * POOL_GROUP_DIM], w_grp_ref[g])
         for g in range(POOL_GROUPS)], axis=-1)
    br_pool = _dot((pg * pool_scale * zp_act).astype(BF16), w_brp_ref[...])
    br_attn = _dot(att_act, w_bra_ref[...])
    merged = sig_gp * br_pool + sig_ga * br_attn
    out = x + _dot(merged.astype(BF16), w_out_ref[...])
    return _rmsnorm(out, g_final)


def _cast_kernel(w_in_ref, w_grp_ref, w_brp_ref, w_bra_ref, w_out_ref,
                 o_in_ref, o_grp_ref, o_brp_ref, o_bra_ref, o_out_ref):
    o_in_ref[...] = w_in_ref[0].astype(BF16)
    o_grp_ref[...] = w_grp_ref[0].astype(BF16)
    o_brp_ref[...] = w_brp_ref[0].astype(BF16)
    o_bra_ref[...] = w_bra_ref[0].astype(BF16)
    o_out_ref[...] = w_out_ref[0].astype(BF16)


def _cast_weights(w_in, w_grp, w_brp, w_bra, w_out):
    def rows(n):
        assert n % (CAST_STEPS * 2 * SUBLANES) == 0
        return n // CAST_STEPS

    r_in, r_grp, r_br, r_out = rows(D_MODEL), rows(POOL_GROUP_DIM), rows(POOL_WIDTH), rows(D_MODEL)
    in_specs = [
        pl.BlockSpec((1, r_in, IN_COLS), lambda s: (0, s, 0)),
        pl.BlockSpec((1, POOL_GROUPS, r_grp, POOL_GROUP_DIM), lambda s: (0, 0, s, 0)),
        pl.BlockSpec((1, r_br, D_MODEL), lambda s: (0, s, 0)),
        pl.BlockSpec((1, r_br, D_MODEL), lambda s: (0, s, 0)),
        pl.BlockSpec((1, r_out, D_MODEL), lambda s: (0, s, 0)),
    ]
    out_specs = (
        pl.BlockSpec((r_in, IN_COLS), lambda s: (s, 0)),
        pl.BlockSpec((POOL_GROUPS, r_grp, POOL_GROUP_DIM), lambda s: (0, s, 0)),
        pl.BlockSpec((r_br, D_MODEL), lambda s: (s, 0)),
        pl.BlockSpec((r_br, D_MODEL), lambda s: (s, 0)),
        pl.BlockSpec((r_out, D_MODEL), lambda s: (s, 0)),
    )
    out_shape = (
        jax.ShapeDtypeStruct((D_MODEL, IN_COLS), BF16),
        jax.ShapeDtypeStruct((POOL_GROUPS, POOL_GROUP_DIM, POOL_GROUP_DIM), BF16),
        jax.ShapeDtypeStruct((POOL_WIDTH, D_MODEL), BF16),
        jax.ShapeDtypeStruct((ATTN_WIDTH, D_MODEL), BF16),
        jax.ShapeDtypeStruct((D_MODEL, D_MODEL), BF16),
    )
    return pl.pallas_call(
        _cast_kernel, grid=(CAST_STEPS,), in_specs=in_specs, out_specs=out_specs, out_shape=out_shape,
        compiler_params=pltpu.CompilerParams(dimension_semantics=("arbitrary",)),
        name="cast_weights",
    )(w_in, w_grp, w_brp, w_bra, w_out)


def _prompt_kernel(relb_ref, sinks_ref, x_ref, gnorm_ref, w_in_ref, w_grp_ref, pscale_ref, w_brp_ref,
                   w_bra_ref, w_out_ref, gfinal_ref,
                   y_ref, knew_ref, vnew_ref, pnew_ref,
                   xn_s, uext_s, zp_s, qh_s, kband_s, vband_s, za_s, gp_s, ga_s, att_s, bias_s):
    b = pl.program_id(0)
    j = pl.program_id(1)
    nblk = TM // WINDOW
    hrows = N_Q_HEADS * WINDOW

    @pl.when((b == 0) & (j == 0))
    def _build_bias():
        qi = lax.broadcasted_iota(jnp.int32, (WINDOW, 2 * WINDOW), 0)
        kc = lax.broadcasted_iota(jnp.int32, (WINDOW, 2 * WINDOW), 1)
        for h in range(N_Q_HEADS):
            bias_s[h * WINDOW:(h + 1) * WINDOW, :] = _bias_table(qi - kc + WINDOW, relb_ref, h)

    @pl.when(j == 0)
    def _reset_carry():
        uext_s[0:POOL_HDR, :] = jnp.zeros((POOL_HDR, POOL_WIDTH), F32)
        kband_s[0:WINDOW, :] = jnp.zeros((WINDOW, KV_WIDTH), BF16)
        vband_s[0:WINDOW, :] = jnp.zeros((WINDOW, KV_WIDTH), BF16)

    xn_s[...] = _rmsnorm(x_ref[0], gnorm_ref[...]).astype(BF16)
    uext_s[POOL_HDR:, :] = _project(xn_s, w_in_ref, OFF_U, POOL_WIDTH)
    zp_s[...] = _silu(_project(xn_s, w_in_ref, OFF_ZP, POOL_WIDTH))
    q_heads = _head_blocks(_project(xn_s, w_in_ref, OFF_Q, ATTN_WIDTH) * (HEAD_DIM ** -0.5))
    for c in range(nblk):
        for h in range(N_Q_HEADS):
            r0 = c * hrows + h * WINDOW
            qh_s[r0:r0 + WINDOW, :] = q_heads[h][c * WINDOW:(c + 1) * WINDOW].astype(BF16)
    kv = _project(xn_s, w_in_ref, OFF_K, 2 * KV_WIDTH)
    kband_s[WINDOW:, :] = kv[:, :KV_WIDTH].astype(BF16)
    vband_s[WINDOW:, :] = kv[:, KV_WIDTH:].astype(BF16)
    knew_ref[0] = kv[TM - WINDOW:, :KV_WIDTH].T
    vnew_ref[0] = kv[TM - WINDOW:, KV_WIDTH:].T
    za_s[...] = _silu(_project(xn_s, w_in_ref, OFF_ZA, ATTN_WIDTH))
    for half in range(2):
        cols = slice(half * POOL_WIDTH, (half + 1) * POOL_WIDTH)
        gp_s[:, cols] = _sigmoid(_project(xn_s, w_in_ref, OFF_GP + half * POOL_WIDTH, POOL_WIDTH))
        ga_s[:, cols] = _sigmoid(_project(xn_s, w_in_ref, OFF_GA + half * POOL_WIDTH, POOL_WIDTH))

    pos = j * TM + lax.broadcasted_iota(jnp.int32, (TM, 1), 0)
    inv_cnt = [1.0 / jnp.minimum(pos + 1, w).astype(F32) for w in POOL_WINDOWS]
    x_ext = uext_s[...]
    pooled = _pool_windows(x_ext, inv_cnt)
    pnew_ref[0] = x_ext[POOL_HDR + TM - POOL_STATE_ROWS:, :]
    uext_s[0:POOL_HDR, :] = x_ext[TM:, :]

    first_neg = jnp.where(j == 0, NEG_INF, 0.0).astype(F32)
    for c in range(nblk):
        rows = slice(c * WINDOW, (c + 1) * WINDOW)
        k_c = kband_s[c * WINDOW:(c + 2) * WINDOW, :]
        v_c = vband_s[c * WINDOW:(c + 2) * WINDOW, :]
        s_all = lax.dot_general(qh_s[c * hrows:(c + 1) * hrows, :], k_c, (((1,), (1,)), ((), ())),
                                preferred_element_type=F32)
        p_parts, inv_parts = [], []
        for h in range(N_Q_HEADS):
            head_rows = slice(h * WINDOW, (h + 1) * WINDOW)
            s = s_all[head_rows, :] + bias_s[head_rows, :]
            if c == 0:
                s = jnp.concatenate([s[:, :WINDOW] + first_neg, s[:, WINDOW:]], axis=-1)
            sink = sinks_ref[h]
            m = jnp.maximum(jnp.max(s, axis=-1, keepdims=True), sink)
            p = jnp.exp(s - m)
            denom = jnp.sum(p, axis=-1, keepdims=True) + jnp.exp(sink - m)
            p_parts.append(p.astype(BF16))
            inv_parts.append(1.0 / denom)
        o_all = _dot(jnp.concatenate(p_parts, axis=0), v_c)
        outs = [o_all[h * WINDOW:(h + 1) * WINDOW, :] * inv_parts[h] for h in range(N_Q_HEADS)]
        att_s[rows, :] = (_gather_heads(outs) * za_s[rows, :]).astype(BF16)
    kband_s[0:WINDOW, :] = kband_s[TM:, :]
    vband_s[0:WINDOW, :] = vband_s[TM:, :]

    y_ref[0] = _merge(x_ref[0], pooled, zp_s[...], att_s[...], gp_s[...], ga_s[...], w_grp_ref,
                      pscale_ref[...], w_brp_ref, w_bra_ref, w_out_ref, gfinal_ref[...])


def _resident(shape):
    return pl.BlockSpec(shape, lambda *_: (0,) * len(shape), pipeline_mode=pl.Buffered(1))


_SMEM = pl.BlockSpec(memory_space=pltpu.SMEM)


def _weight_specs():
    return [
        _resident((1, D_MODEL)),
        _resident((D_MODEL, IN_COLS)),
        _resident((POOL_GROUPS, POOL_GROUP_DIM, POOL_GROUP_DIM)),
        _resident((1, POOL_WIDTH)),
        _resident((POOL_WIDTH, D_MODEL)),
        _resident((ATTN_WIDTH, D_MODEL)),
        _resident((D_MODEL, D_MODEL)),
        _resident((1, D_MODEL)),
    ]


def _prompt_call(x, relb, sinks, weights):
    batch, seq, _ = x.shape
    assert seq % TM == 0 and TM % WINDOW == 0 and TM >= POOL_HDR
    grid = (batch, seq // TM)
    out_shape = (
        jax.ShapeDtypeStruct((batch, seq, D_MODEL), F32),
        jax.ShapeDtypeStruct((batch, KV_WIDTH, WINDOW), F32),
        jax.ShapeDtypeStruct((batch, KV_WIDTH, WINDOW), F32),
        jax.ShapeDtypeStruct((batch, POOL_STATE_ROWS, POOL_WIDTH), F32),
    )
    scratch = [
        pltpu.VMEM((TM, D_MODEL), BF16),
        pltpu.VMEM((POOL_HDR + TM, POOL_WIDTH), F32),
        pltpu.VMEM((TM, POOL_WIDTH), F32),
        pltpu.VMEM((TM * N_Q_HEADS, LANES), BF16),
        pltpu.VMEM((WINDOW + TM, KV_WIDTH), BF16),
        pltpu.VMEM((WINDOW + TM, KV_WIDTH), BF16),
        pltpu.VMEM((TM, ATTN_WIDTH), F32),
        pltpu.VMEM((TM, D_MODEL), F32),
        pltpu.VMEM((TM, D_MODEL), F32),
        pltpu.VMEM((TM, ATTN_WIDTH), BF16),
        pltpu.VMEM((N_Q_HEADS * WINDOW, 2 * WINDOW), F32),
    ]
    return pl.pallas_call(
        _prompt_kernel,
        grid=grid,
        in_specs=[_SMEM, _SMEM,
                  pl.BlockSpec((1, TM, D_MODEL), lambda b, j: (b, j, 0))] + _weight_specs(),
        out_specs=(
            pl.BlockSpec((1, TM, D_MODEL), lambda b, j: (b, j, 0)),
            pl.BlockSpec((1, KV_WIDTH, WINDOW), lambda b, j: (b, 0, 0)),
            pl.BlockSpec((1, KV_WIDTH, WINDOW), lambda b, j: (b, 0, 0)),
            pl.BlockSpec((1, POOL_STATE_ROWS, POOL_WIDTH), lambda b, j: (b, 0, 0)),
        ),
        out_shape=out_shape,
        scratch_shapes=scratch,
        compiler_params=pltpu.CompilerParams(
            dimension_semantics=("arbitrary", "arbitrary"), vmem_limit_bytes=VMEM_LIMIT_BYTES),
        name="prompt_layer",
    )(relb, sinks, x, *weights)


def _sample_kernel(relb_ref, sinks_ref, x_ref, kt_ref, vt_ref, st_ref,
                   gnorm_ref, w_in_ref, w_grp_ref, pscale_ref, w_brp_ref, w_bra_ref, w_out_ref, gfinal_ref,
                   y_ref, knew_ref, vnew_ref, pnew_ref,
                   xn_s, u_s, zp_s, qh_s, k_s, v_s, za_s, gp_s, ga_s, pooled_s, att_s,
                   q2_s, kn2_s, vn2_s, o2_s, bias_s):
    i = pl.program_id(0)
    n_steps = pl.num_programs(0)
    chunk = pl.ds(pl.multiple_of(i * BC, BC), BC)

    @pl.when(i == 0)
    def _project_all():
        slot = lax.broadcasted_iota(jnp.int32, (SUBLANES, WINDOW), 1)
        head = lax.broadcasted_iota(jnp.int32, (SUBLANES, WINDOW), 0)
        tbl = jnp.zeros((SUBLANES, WINDOW), F32)
        for h in range(N_Q_HEADS):
            tbl = jnp.where(head == h, _bias_table(WINDOW - slot, relb_ref, h), tbl)
        bias_s[...] = tbl

        xn_s[...] = _rmsnorm(x_ref[...], gnorm_ref[...]).astype(BF16)
        u_s[...] = _project(xn_s, w_in_ref, OFF_U, POOL_WIDTH)
        zp_s[...] = _silu(_project(xn_s, w_in_ref, OFF_ZP, POOL_WIDTH))
        q_heads = _head_blocks(_project(xn_s, w_in_ref, OFF_Q, ATTN_WIDTH) * (HEAD_DIM ** -0.5))
        for h in range(N_Q_HEADS):
            qh_s[:, h * LANES:(h + 1) * LANES] = q_heads[h]
        kv = _project(xn_s, w_in_ref, OFF_K, 2 * KV_WIDTH)
        k_s[...] = kv[:, :KV_WIDTH]
        v_s[...] = kv[:, KV_WIDTH:]
        za_s[...] = _silu(_project(xn_s, w_in_ref, OFF_ZA, ATTN_WIDTH))
        for half in range(2):
            cols = slice(half * POOL_WIDTH, (half + 1) * POOL_WIDTH)
            gp_s[:, cols] = _sigmoid(_project(xn_s, w_in_ref, OFF_GP + half * POOL_WIDTH, POOL_WIDTH))
            ga_s[:, cols] = _sigmoid(_project(xn_s, w_in_ref, OFF_GA + half * POOL_WIDTH, POOL_WIDTH))

    u_new = u_s[chunk, :]
    n_hist = POOL_STATE_ROWS
    parts = []
    for g, w in enumerate(POOL_WINDOWS):
        cols = slice(g * POOL_GROUP_DIM, (g + 1) * POOL_GROUP_DIM)
        acc = u_new[:, cols]
        for r in range(n_hist - (w - 1), n_hist):
            acc = acc + st_ref[r, :, cols]
        parts.append(acc * (1.0 / w) - u_new[:, cols])
    pooled_s[chunk, :] = jnp.concatenate(parts, axis=-1)
    pnew_ref[0:n_hist - 1] = st_ref[1:n_hist]
    pnew_ref[n_hist - 1] = u_new

    k_new = k_s[chunk, :]
    v_new = v_s[chunk, :]
    last_slot = lax.broadcasted_iota(jnp.int32, (KV_WIDTH, WINDOW), 1) == WINDOW - 1
    k_new_t = k_new.T
    v_new_t = v_new.T
    for s in range(BC):
        knew_ref[s] = jnp.where(last_slot, k_new_t[:, s:s + 1], pltpu.roll(kt_ref[s], WINDOW - 1, 1))
        vnew_ref[s] = jnp.where(last_slot, v_new_t[:, s:s + 1], pltpu.roll(vt_ref[s], WINDOW - 1, 1))

    for h in range(N_Q_HEADS):
        q2_s[pl.ds(h, BC, stride=N_Q_HEADS), :] = qh_s[chunk, h * LANES:(h + 1) * LANES]
        kn2_s[pl.ds(h, BC, stride=N_Q_HEADS), :] = k_new
        vn2_s[pl.ds(h, BC, stride=N_Q_HEADS), :] = v_new
    q3 = q2_s[...].reshape(BC, N_Q_HEADS, LANES)
    kn3 = kn2_s[...].reshape(BC, N_Q_HEADS, LANES)
    vn3 = vn2_s[...].reshape(BC, N_Q_HEADS, LANES)
    s3 = lax.dot_general(q3.astype(BF16), kt_ref[...].astype(BF16), (((2,), (1,)), ((0,), (0,))),
                         preferred_element_type=F32) + bias_s[...][None]
    s_self = jnp.sum(q3 * kn3, axis=-1, keepdims=True) + _per_head_column(lambda h: relb_ref[0, h])[None]
    sink = _per_head_column(lambda h: sinks_ref[h])[None]
    m = jnp.maximum(jnp.maximum(jnp.max(s3, axis=-1, keepdims=True), s_self), sink)
    p3 = jnp.exp(s3 - m)
    p_self = jnp.exp(s_self - m)
    denom = jnp.sum(p3, axis=-1, keepdims=True) + p_self + jnp.exp(sink - m)
    o3 = lax.dot_general(p3.astype(BF16), vt_ref[...].astype(BF16), (((2,), (2,)), ((0,), (0,))),
                         preferred_element_type=F32)
    o3 = (o3 + p_self * vn3) * (1.0 / denom)
    o2_s[...] = o3.reshape(BC * N_Q_HEADS, LANES)
    outs = [o2_s[pl.ds(h, BC, stride=N_Q_HEADS), :] for h in range(N_Q_HEADS)]
    att_s[chunk, :] = (_gather_heads(outs) * za_s[chunk, :]).astype(BF16)

    @pl.when(i == n_steps - 1)
    def _merge_all():
        y_ref[...] = _merge(x_ref[...], pooled_s[...], zp_s[...], att_s[...], gp_s[...], ga_s[...],
                            w_grp_ref, pscale_ref[...], w_brp_ref, w_bra_ref, w_out_ref, gfinal_ref[...])


def _sample_call(x, cache_kt, cache_vt, state_t, relb, sinks, weights):
    n_seq = x.shape[0]
    assert n_seq % BC == 0
    out_shape = (
        jax.ShapeDtypeStruct((n_seq, D_MODEL), F32),
        jax.ShapeDtypeStruct((n_seq, KV_WIDTH, WINDOW), F32),
        jax.ShapeDtypeStruct((n_seq, KV_WIDTH, WINDOW), F32),
        jax.ShapeDtypeStruct((POOL_STATE_ROWS, n_seq, POOL_WIDTH), F32),
    )
    scratch = [
        pltpu.VMEM((n_seq, D_MODEL), BF16),
        pltpu.VMEM((n_seq, POOL_WIDTH), F32),
        pltpu.VMEM((n_seq, POOL_WIDTH), F32),
        pltpu.VMEM((n_seq, N_Q_HEADS * LANES), F32),
        pltpu.VMEM((n_seq, KV_WIDTH), F32),
        pltpu.VMEM((n_seq, KV_WIDTH), F32),
        pltpu.VMEM((n_seq, ATTN_WIDTH), F32),
        pltpu.VMEM((n_seq, D_MODEL), F32),
        pltpu.VMEM((n_seq, D_MODEL), F32),
        pltpu.VMEM((n_seq, POOL_WIDTH), F32),
        pltpu.VMEM((n_seq, ATTN_WIDTH), BF16),
        pltpu.VMEM((BC * N_Q_HEADS, LANES), F32),
        pltpu.VMEM((BC * N_Q_HEADS, LANES), F32),
        pltpu.VMEM((BC * N_Q_HEADS, LANES), F32),
        pltpu.VMEM((BC * N_Q_HEADS, LANES), F32),
        pltpu.VMEM((SUBLANES, WINDOW), F32),
    ]
    cache = pl.BlockSpec((BC, KV_WIDTH, WINDOW), lambda i: (i, 0, 0))
    hist = pl.BlockSpec((POOL_STATE_ROWS, BC, POOL_WIDTH), lambda i: (0, i, 0))
    return pl.pallas_call(
        _sample_kernel,
        grid=(n_seq // BC,),
        in_specs=[_SMEM, _SMEM, _resident((n_seq, D_MODEL)), cache, cache, hist] + _weight_specs(),
        out_specs=(pl.BlockSpec((n_seq, D_MODEL), lambda i: (0, 0)), cache, cache, hist),
        out_shape=out_shape,
        scratch_shapes=scratch,
        compiler_params=pltpu.CompilerParams(
            dimension_semantics=("arbitrary",), vmem_limit_bytes=VMEM_LIMIT_BYTES),
        name="sample_layer",
    )(relb, sinks, x, cache_kt, cache_vt, state_t, *weights)


def _cache_as_kd_slot(cache):
    n = cache.shape[0]
    return cache.transpose(0, 2, 3, 1).reshape(n, KV_WIDTH, WINDOW)


def _cache_from_kd_slot(cache_t):
    n = cache_t.shape[0]
    return cache_t.reshape(n, N_KV_HEADS, HEAD_DIM, WINDOW).transpose(0, 3, 1, 2)


def kernel(x_prompt, x_sample, cache_k, cache_v, state_pool, rel_bias, g_norm, w_in, pool_w_grp, pool_scale, attn_sinks, w_br_pool, w_br_attn, w_out, g_final):
    depth = g_norm.shape[0]
    assert depth == 1 and x_sample.shape[1] == 1
    l = 0
    w_in_bf, w_grp_bf, w_brp_bf, w_bra_bf, w_out_bf = _cast_weights(w_in, pool_w_grp, w_br_pool, w_br_attn, w_out)
    weights = (g_norm[l].reshape(1, D_MODEL), w_in_bf, w_grp_bf, pool_scale[l].reshape(1, POOL_WIDTH),
               w_brp_bf, w_bra_bf, w_out_bf, g_final.reshape(1, D_MODEL))
    sinks = attn_sinks[l]
    n_seq = x_sample.shape[0]

    y_p, k_p, v_p, pool_p = _prompt_call(x_prompt, rel_bias, sinks, weights)
    y_s, k_s, v_s, pool_s = _sample_call(
        x_sample.reshape(n_seq, D_MODEL), _cache_as_kd_slot(cache_k[l]), _cache_as_kd_slot(cache_v[l]),
        state_pool[l].transpose(1, 0, 2), rel_bias, sinks, weights)
    return (
        y_p,
        y_s.reshape(n_seq, 1, D_MODEL),
        _cache_from_kd_slot(k_p)[None],
        _cache_from_kd_slot(v_p)[None],
        pool_p[None],
        _cache_from_kd_slot(k_s)[None],
        _cache_from_kd_slot(v_s)[None],
        pool_s.transpose(1, 0, 2)[None],
    )
```

```python
import math

import jax
import jax.numpy as jnp
from jax import lax
from jax.experimental import pallas as pl
from jax.experimental.pallas import tpu as pltpu

D_MODEL = 1024
POOL_WINDOWS = (2, 4, 8, 16)
POOL_GROUPS = len(POOL_WINDOWS)
POOL_WIDTH = D_MODEL // 2
POOL_GROUP_DIM = POOL_WIDTH // POOL_GROUPS
POOL_STATE_ROWS = max(POOL_WINDOWS) - 1
HEAD_DIM = 64
N_KV_HEADS = 2
ATTN_WIDTH = D_MODEL // 2
N_Q_HEADS = ATTN_WIDTH // HEAD_DIM
GQA_GROUP = N_Q_HEADS // N_KV_HEADS
KV_WIDTH = N_KV_HEADS * HEAD_DIM
WINDOW = 128
N_BUCKETS = 32
MAX_DISTANCE = 128
RMS_EPS = 1e-6
IN_COLS = 2 * POOL_WIDTH + 2 * ATTN_WIDTH + 2 * KV_WIDTH + 2 * D_MODEL

OFF_U = 0
OFF_ZP = OFF_U + POOL_WIDTH
OFF_Q = OFF_ZP + POOL_WIDTH
OFF_K = OFF_Q + ATTN_WIDTH
OFF_V = OFF_K + KV_WIDTH
OFF_ZA = OFF_V + KV_WIDTH
OFF_GP = OFF_ZA + ATTN_WIDTH
OFF_GA = OFF_GP + D_MODEL

LANES = 128
SUBLANES = 8
POOL_HDR = 16
TM = 512
BC = 32
CAST_STEPS = 8
VMEM_LIMIT_BYTES = 56 * 1024 * 1024

assert KV_WIDTH == LANES and 2 * HEAD_DIM == LANES and GQA_GROUP % 2 == 0

F32 = jnp.float32
BF16 = jnp.bfloat16
NEG_INF = float("-inf")


def _sigmoid(x):
    return 1.0 / (1.0 + jnp.exp(-x))


def _silu(x):
    return x * _sigmoid(x)


def _rmsnorm(x, g):
    return x * lax.rsqrt(jnp.mean(x * x, axis=-1, keepdims=True) + RMS_EPS) * g


def _dot(a, b):
    return jnp.dot(a, b, preferred_element_type=F32)


def _t5_bucket(rel):
    n = jnp.maximum(rel, 0)
    max_exact = N_BUCKETS // 2
    nf = jnp.maximum(n, 1).astype(F32)
    v = jnp.log(nf / max_exact) / math.log(MAX_DISTANCE / max_exact) * (N_BUCKETS - max_exact)
    steps = jnp.zeros(rel.shape, jnp.int32)
    for k in range(1, N_BUCKETS - max_exact):
        steps = steps + jnp.where(v >= k, 1, 0)
    return jnp.where(n < max_exact, n, max_exact + steps)


def _bias_table(rel, relb_ref, h):
    bucket = _t5_bucket(rel)

    def body(b, tbl):
        return jnp.where(bucket == b, relb_ref[b, h], tbl)

    tbl = lax.fori_loop(0, N_BUCKETS, body, jnp.zeros(rel.shape, F32))
    return jnp.where((rel >= 0) & (rel < WINDOW), tbl, NEG_INF)


def _per_head_column(ref_scalar):
    head = lax.broadcasted_iota(jnp.int32, (N_Q_HEADS, 1), 0)
    col = jnp.zeros((N_Q_HEADS, 1), F32)
    for h in range(N_Q_HEADS):
        col = jnp.where(head == h, ref_scalar(h), col)
    return col


def _project(xn_ref, w_in_ref, off, width):
    return _dot(xn_ref[...], w_in_ref[:, off:off + width])


def _kv_lane_mask(shape):
    return lax.broadcasted_iota(jnp.int32, shape, len(shape) - 1) < HEAD_DIM


def _head_pairs():
    for grp in range(ATTN_WIDTH // LANES):
        kv, pair = divmod(grp, GQA_GROUP // 2)
        yield grp, kv, kv * GQA_GROUP + 2 * pair, kv * GQA_GROUP + 2 * pair + 1


def _head_blocks(q):
    low = _kv_lane_mask((q.shape[0], LANES))
    blocks = [None] * N_Q_HEADS
    for grp, kv, even, odd in _head_pairs():
        a = q[:, grp * LANES:(grp + 1) * LANES]
        swapped = pltpu.roll(a, HEAD_DIM, 1)
        if kv == 0:
            blocks[even], blocks[odd] = jnp.where(low, a, 0.0), jnp.where(low, swapped, 0.0)
        else:
            blocks[even], blocks[odd] = jnp.where(low, 0.0, swapped), jnp.where(low, 0.0, a)
    return blocks


def _gather_heads(outs):
    low = _kv_lane_mask(outs[0].shape)
    groups = []
    for _, kv, even, odd in _head_pairs():
        if kv == 0:
            groups.append(jnp.where(low, outs[even], pltpu.roll(outs[odd], HEAD_DIM, 1)))
        else:
            groups.append(jnp.where(low, pltpu.roll(outs[even], HEAD_DIM, 1), outs[odd]))
    return jnp.concatenate(groups, axis=-1)


def _pool_windows(x_ext, inv_cnt):
    outs = []
    for g, w in enumerate(POOL_WINDOWS):
        xg = x_ext[:, g * POOL_GROUP_DIM:(g + 1) * POOL_GROUP_DIM]
        acc = xg
        shift = 1
        while shift < w:
            acc = acc + pltpu.roll(acc, shift, 0)
            shift *= 2
        outs.append(acc[POOL_HDR:] * inv_cnt[g] - xg[POOL_HDR:])
    return jnp.concatenate(outs, axis=-1)


def _merge(x, pooled, zp_act, att_act, sig_gp, sig_ga, w_grp_ref, pool_scale, w_brp_ref, w_bra_ref,
           w_out_ref, g_final):
    pg = jnp.concatenate(
        [_dot(pooled[:, g * POOL_GROUP_DIM:(g + 1) * POOL_GROUP_DIM], w_grp_ref[g])
         for g in range(POOL_GROUPS)], axis=-1)
    br_pool = _dot((pg * pool_scale * zp_act).astype(BF16), w_brp_ref[...])
    br_attn = _dot(att_act, w_bra_ref[...])
    merged = sig_gp * br_pool + sig_ga * br_attn
    out = x + _dot(merged.astype(BF16), w_out_ref[...])
    return _rmsnorm(out, g_final)


def _cast_kernel(w_in_ref, w_grp_ref, w_brp_ref, w_bra_ref, w_out_ref,
                 o_in_ref, o_grp_ref, o_brp_ref, o_bra_ref, o_out_ref):
    o_in_ref[...] = w_in_ref[0].astype(BF16)
    o_grp_ref[...] = w_grp_ref[0].astype(BF16)
    o_brp_ref[...] = w_brp_ref[0].astype(BF16)
    o_bra_ref[...] = w_bra_ref[0].astype(BF16)
    o_out_ref[...] = w_out_ref[0].astype(BF16)


def _cast_weights(w_in, w_grp, w_brp, w_bra, w_out):
    def rows(n):
        assert n % (CAST_STEPS * 2 * SUBLANES) == 0
        return n // CAST_STEPS

    r_in, r_grp, r_br, r_out = rows(D_MODEL), rows(POOL_GROUP_DIM), rows(POOL_WIDTH), rows(D_MODEL)
    in_specs = [
        pl.BlockSpec((1, r_in, IN_COLS), lambda s: (0, s, 0)),
        pl.BlockSpec((1, POOL_GROUPS, r_grp, POOL_GROUP_DIM), lambda s: (0, 0, s, 0)),
        pl.BlockSpec((1, r_br, D_MODEL), lambda s: (0, s, 0)),
        pl.BlockSpec((1, r_br, D_MODEL), lambda s: (0, s, 0)),
        pl.BlockSpec((1, r_out, D_MODEL), lambda s: (0, s, 0)),
    ]
    out_specs = (
        pl.BlockSpec((r_in, IN_COLS), lambda s: (s, 0)),
        pl.BlockSpec((POOL_GROUPS, r_grp, POOL_GROUP_DIM), lambda s: (0, s, 0)),
        pl.BlockSpec((r_br, D_MODEL), lambda s: (s, 0)),
        pl.BlockSpec((r_br, D_MODEL), lambda s: (s, 0)),
        pl.BlockSpec((r_out, D_MODEL), lambda s: (s, 0)),
    )
    out_shape = (
        jax.ShapeDtypeStruct((D_MODEL, IN_COLS), BF16),
        jax.ShapeDtypeStruct((POOL_GROUPS, POOL_GROUP_DIM, POOL_GROUP_DIM), BF16),
        jax.ShapeDtypeStruct((POOL_WIDTH, D_MODEL), BF16),
        jax.ShapeDtypeStruct((ATTN_WIDTH, D_MODEL), BF16),
        jax.ShapeDtypeStruct((D_MODEL, D_MODEL), BF16),
    )
    return pl.pallas_call(
        _cast_kernel, grid=(CAST_STEPS,), in_specs=in_specs, out_specs=out_specs, out_shape=out_shape,
        compiler_params=pltpu.CompilerParams(dimension_semantics=("arbitrary",)),
        name="cast_weights",
    )(w_in, w_grp, w_brp, w_bra, w_out)


def _prompt_kernel(relb_ref, sinks_ref, x_ref, gnorm_ref, w_in_ref, w_grp_ref, pscale_ref, w_brp_ref,
                   w_bra_ref, w_out_ref, gfinal_ref,
                   y_ref, knew_ref, vnew_ref, pnew_ref,
                   xn_s, uext_s, zp_s, qh_s, kband_s, vband_s, za_s, gp_s, ga_s, pooled_s, att_s, bias_s):
    b = pl.program_id(0)
    j = pl.program_id(1)
    nblk = TM // WINDOW
    hrows = N_Q_HEADS * WINDOW

    @pl.when((b == 0) & (j == 0))
    def _build_bias():
        qi = lax.broadcasted_iota(jnp.int32, (WINDOW, 2 * WINDOW), 0)
        kc = lax.broadcasted_iota(jnp.int32, (WINDOW, 2 * WINDOW), 1)
        for h in range(N_Q_HEADS):
            bias_s[h * WINDOW:(h + 1) * WINDOW, :] = _bias_table(qi - kc + WINDOW, relb_ref, h)

    @pl.when(j == 0)
    def _reset_carry():
        uext_s[0:POOL_HDR, :] = jnp.zeros((POOL_HDR, POOL_WIDTH), F32)
        kband_s[0:WINDOW, :] = jnp.zeros((WINDOW, KV_WIDTH), BF16)
        vband_s[0:WINDOW, :] = jnp.zeros((WINDOW, KV_WIDTH), BF16)

    xn_s[...] = _rmsnorm(x_ref[0], gnorm_ref[...]).astype(BF16)
    uext_s[POOL_HDR:, :] = _project(xn_s, w_in_ref, OFF_U, POOL_WIDTH)
    q_heads = _head_blocks(_project(xn_s, w_in_ref, OFF_Q, ATTN_WIDTH) * (HEAD_DIM ** -0.5))
    for c in range(nblk):
        for h in range(N_Q_HEADS):
            r0 = c * hrows + h * WINDOW
            qh_s[r0:r0 + WINDOW, :] = q_heads[h][c * WINDOW:(c + 1) * WINDOW].astype(BF16)

    pos = j * TM + lax.broadcasted_iota(jnp.int32, (TM, 1), 0)
    inv_cnt = [1.0 / jnp.minimum(pos + 1, w).astype(F32) for w in POOL_WINDOWS]
    x_ext = uext_s[...]
    pooled_s[...] = _pool_windows(x_ext, inv_cnt).astype(BF16)
    pnew_ref[0] = x_ext[POOL_HDR + TM - POOL_STATE_ROWS:, :]
    uext_s[0:POOL_HDR, :] = x_ext[TM:, :]

    kv = _project(xn_s, w_in_ref, OFF_K, 2 * KV_WIDTH)
    kband_s[WINDOW:, :] = kv[:, :KV_WIDTH].astype(BF16)
    vband_s[WINDOW:, :] = kv[:, KV_WIDTH:].astype(BF16)
    knew_ref[0] = kv[TM - WINDOW:, :KV_WIDTH].T
    vnew_ref[0] = kv[TM - WINDOW:, KV_WIDTH:].T
    za_s[...] = _silu(_project(xn_s, w_in_ref, OFF_ZA, ATTN_WIDTH))

    first_neg = jnp.where(j == 0, NEG_INF, 0.0).astype(F32)
    for c in range(nblk):
        rows = slice(c * WINDOW, (c + 1) * WINDOW)
        k_c = kband_s[c * WINDOW:(c + 2) * WINDOW, :]
        v_c = vband_s[c * WINDOW:(c + 2) * WINDOW, :]
        s_all = lax.dot_general(qh_s[c * hrows:(c + 1) * hrows, :], k_c, (((1,), (1,)), ((), ())),
                                preferred_element_type=F32)
        p_parts, inv_parts = [], []
        for h in range(N_Q_HEADS):
            head_rows = slice(h * WINDOW, (h + 1) * WINDOW)
            s = s_all[head_rows, :] + bias_s[head_rows, :]
            if c == 0:
                s = jnp.concatenate([s[:, :WINDOW] + first_neg, s[:, WINDOW:]], axis=-1)
            sink = sinks_ref[h]
            m = jnp.maximum(jnp.max(s, axis=-1, keepdims=True), sink)
            p = jnp.exp(s - m)
            denom = jnp.sum(p, axis=-1, keepdims=True) + jnp.exp(sink - m)
            p_parts.append(p.astype(BF16))
            inv_parts.append(1.0 / denom)
        o_all = _dot(jnp.concatenate(p_parts, axis=0), v_c)
        outs = [o_all[h * WINDOW:(h + 1) * WINDOW, :] * inv_parts[h] for h in range(N_Q_HEADS)]
        att_s[rows, :] = (_gather_heads(outs) * za_s[rows, :]).astype(BF16)
    kband_s[0:WINDOW, :] = kband_s[TM:, :]
    vband_s[0:WINDOW, :] = vband_s[TM:, :]

    zp_s[...] = _silu(_project(xn_s, w_in_ref, OFF_ZP, POOL_WIDTH))
    for half in range(2):
        cols = slice(half * POOL_WIDTH, (half + 1) * POOL_WIDTH)
        gp_s[:, cols] = _sigmoid(_project(xn_s, w_in_ref, OFF_GP + half * POOL_WIDTH, POOL_WIDTH))
        ga_s[:, cols] = _sigmoid(_project(xn_s, w_in_ref, OFF_GA + half * POOL_WIDTH, POOL_WIDTH))

    y_ref[0] = _merge(x_ref[0], pooled_s[...], zp_s[...], att_s[...], gp_s[...], ga_s[...], w_grp_ref,
                      pscale_ref[...], w_brp_ref, w_bra_ref, w_out_ref, gfinal_ref[...])


def _resident(shape):
    return pl.BlockSpec(shape, lambda *_: (0,) * len(shape), pipeline_mode=pl.Buffered(1))


_SMEM = pl.BlockSpec(memory_space=pltpu.SMEM)


def _weight_specs():
    return [
        _resident((1, D_MODEL)),
        _resident((D_MODEL, IN_COLS)),
        _resident((POOL_GROUPS, POOL_GROUP_DIM, POOL_GROUP_DIM)),
        _resident((1, POOL_WIDTH)),
        _resident((POOL_WIDTH, D_MODEL)),
        _resident((ATTN_WIDTH, D_MODEL)),
        _resident((D_MODEL, D_MODEL)),
        _resident((1, D_MODEL)),
    ]


def _prompt_call(x, relb, sinks, weights):
    batch, seq, _ = x.shape
    assert seq % TM == 0 and TM % WINDOW == 0 and TM >= POOL_HDR
    grid = (batch, seq // TM)
    out_shape = (
        jax.ShapeDtypeStruct((batch, seq, D_MODEL), F32),
        jax.ShapeDtypeStruct((batch, KV_WIDTH, WINDOW), F32),
        jax.ShapeDtypeStruct((batch, KV_WIDTH, WINDOW), F32),
        jax.ShapeDtypeStruct((batch, POOL_STATE_ROWS, POOL_WIDTH), F32),
    )
    scratch = [
        pltpu.VMEM((TM, D_MODEL), BF16),
        pltpu.VMEM((POOL_HDR + TM, POOL_WIDTH), F32),
        pltpu.VMEM((TM, POOL_WIDTH), F32),
        pltpu.VMEM((TM * N_Q_HEADS, LANES), BF16),
        pltpu.VMEM((WINDOW + TM, KV_WIDTH), BF16),
        pltpu.VMEM((WINDOW + TM, KV_WIDTH), BF16),
        pltpu.VMEM((TM, ATTN_WIDTH), F32),
        pltpu.VMEM((TM, D_MODEL), F32),
        pltpu.VMEM((TM, D_MODEL), F32),
        pltpu.VMEM((TM, POOL_WIDTH), BF16),
        pltpu.VMEM((TM, ATTN_WIDTH), BF16),
        pltpu.VMEM((N_Q_HEADS * WINDOW, 2 * WINDOW), F32),
    ]
    return pl.pallas_call(
        _prompt_kernel,
        grid=grid,
        in_specs=[_SMEM, _SMEM,
                  pl.BlockSpec((1, TM, D_MODEL), lambda b, j: (b, j, 0))] + _weight_specs(),
        out_specs=(
            pl.BlockSpec((1, TM, D_MODEL), lambda b, j: (b, j, 0)),
            pl.BlockSpec((1, KV_WIDTH, WINDOW), lambda b, j: (b, 0, 0)),
            pl.BlockSpec((1, KV_WIDTH, WINDOW), lambda b, j: (b, 0, 0)),
            pl.BlockSpec((1, POOL_STATE_ROWS, POOL_WIDTH), lambda b, j: (b, 0, 0)),
        ),
        out_shape=out_shape,
        scratch_shapes=scratch,
        compiler_params=pltpu.CompilerParams(
            dimension_semantics=("arbitrary", "arbitrary"), vmem_limit_bytes=VMEM_LIMIT_BYTES),
        name="prompt_layer",
    )(relb, sinks, x, *weights)


def _sample_kernel(relb_ref, sinks_ref, x_ref, kt_ref, vt_ref, st_ref,
                   gnorm_ref, w_in_ref, w_grp_ref, pscale_ref, w_brp_ref, w_bra_ref, w_out_ref, gfinal_ref,
                   y_ref, knew_ref, vnew_ref, pnew_ref,
                   xn_s, u_s, zp_s, qh_s, k_s, v_s, za_s, gp_s, ga_s, pooled_s, att_s,
                   q2_s, kn2_s, vn2_s, o2_s, bias_s):
    i = pl.program_id(0)
    n_steps = pl.num_programs(0)
    chunk = pl.ds(pl.multiple_of(i * BC, BC), BC)

    @pl.when(i == 0)
    def _project_all():
        slot = lax.broadcasted_iota(jnp.int32, (SUBLANES, WINDOW), 1)
        head = lax.broadcasted_iota(jnp.int32, (SUBLANES, WINDOW), 0)
        tbl = jnp.zeros((SUBLANES, WINDOW), F32)
        for h in range(N_Q_HEADS):
            tbl = jnp.where(head == h, _bias_table(WINDOW - slot, relb_ref, h), tbl)
        bias_s[...] = tbl

        xn_s[...] = _rmsnorm(x_ref[...], gnorm_ref[...]).astype(BF16)
        u_s[...] = _project(xn_s, w_in_ref, OFF_U, POOL_WIDTH)
        zp_s[...] = _silu(_project(xn_s, w_in_ref, OFF_ZP, POOL_WIDTH))
        q_heads = _head_blocks(_project(xn_s, w_in_ref, OFF_Q, ATTN_WIDTH) * (HEAD_DIM ** -0.5))
        for h in range(N_Q_HEADS):
            qh_s[:, h * LANES:(h + 1) * LANES] = q_heads[h]
        kv = _project(xn_s, w_in_ref, OFF_K, 2 * KV_WIDTH)
        k_s[...] = kv[:, :KV_WIDTH]
        v_s[...] = kv[:, KV_WIDTH:]
        za_s[...] = _silu(_project(xn_s, w_in_ref, OFF_ZA, ATTN_WIDTH))
        for half in range(2):
            cols = slice(half * POOL_WIDTH, (half + 1) * POOL_WIDTH)
            gp_s[:, cols] = _sigmoid(_project(xn_s, w_in_ref, OFF_GP + half * POOL_WIDTH, POOL_WIDTH))
            ga_s[:, cols] = _sigmoid(_project(xn_s, w_in_ref, OFF_GA + half * POOL_WIDTH, POOL_WIDTH))

    u_new = u_s[chunk, :]
    n_hist = POOL_STATE_ROWS
    parts = []
    for g, w in enumerate(POOL_WINDOWS):
        cols = slice(g * POOL_GROUP_DIM, (g + 1) * POOL_GROUP_DIM)
        acc = u_new[:, cols]
        for r in range(n_hist - (w - 1), n_hist):
            acc = acc + st_ref[r, :, cols]
        parts.append(acc * (1.0 / w) - u_new[:, cols])
    pooled_s[chunk, :] = jnp.concatenate(parts, axis=-1).astype(BF16)
    pnew_ref[0:n_hist - 1] = st_ref[1:n_hist]
    pnew_ref[n_hist - 1] = u_new

    k_new = k_s[chunk, :]
    v_new = v_s[chunk, :]
    last_slot = lax.broadcasted_iota(jnp.int32, (KV_WIDTH, WINDOW), 1) == WINDOW - 1
    k_new_t = k_new.T
    v_new_t = v_new.T
    for s in range(BC):
        knew_ref[s] = jnp.where(last_slot, k_new_t[:, s:s + 1], pltpu.roll(kt_ref[s], WINDOW - 1, 1))
        vnew_ref[s] = jnp.where(last_slot, v_new_t[:, s:s + 1], pltpu.roll(vt_ref[s], WINDOW - 1, 1))

    for h in range(N_Q_HEADS):
        q2_s[pl.ds(h, BC, stride=N_Q_HEADS), :] = qh_s[chunk, h * LANES:(h + 1) * LANES]
        kn2_s[pl.ds(h, BC, stride=N_Q_HEADS), :] = k_new
        vn2_s[pl.ds(h, BC, stride=N_Q_HEADS), :] = v_new
    q3 = q2_s[...].reshape(BC, N_Q_HEADS, LANES)
    kn3 = kn2_s[...].reshape(BC, N_Q_HEADS, LANES)
    vn3 = vn2_s[...].reshape(BC, N_Q_HEADS, LANES)
    s3 = lax.dot_general(q3.astype(BF16), kt_ref[...].astype(BF16), (((2,), (1,)), ((0,), (0,))),
                         preferred_element_type=F32) + bias_s[...][None]
    s_self = jnp.sum(q3 * kn3, axis=-1, keepdims=True) + _per_head_column(lambda h: relb_ref[0, h])[None]
    sink = _per_head_column(lambda h: sinks_ref[h])[None]
    m = jnp.maximum(jnp.maximum(jnp.max(s3, axis=-1, keepdims=True), s_self), sink)
    p3 = jnp.exp(s3 - m)
    p_self = jnp.exp(s_self - m)
    denom = jnp.sum(p3, axis=-1, keepdims=True) + p_self + jnp.exp(sink - m)
    o3 = lax.dot_general(p3.astype(BF16), vt_ref[...].astype(BF16), (((2,), (2,)), ((0,), (0,))),
                         preferred_element_type=F32)
    o3 = (o3 + p_self * vn3) * (1.0 / denom)
    o2_s[...] = o3.reshape(BC * N_Q_HEADS, LANES)
    outs = [o2_s[pl.ds(h, BC, stride=N_Q_HEADS), :] for h in range(N_Q_HEADS)]
    att_s[chunk, :] = (_gather_heads(outs) * za_s[chunk, :]).astype(BF16)

    @pl.when(i == n_steps - 1)
    def _merge_all():
        y_ref[...] = _merge(x_ref[...], pooled_s[...], zp_s[...], att_s[...], gp_s[...], ga_s[...],
                            w_grp_ref, pscale_ref[...], w_brp_ref, w_bra_ref, w_out_ref, gfinal_ref[...])


def _sample_call(x, cache_kt, cache_vt, state_t, relb, sinks, weights):
    n_seq = x.shape[0]
    assert n_seq % BC == 0
    out_shape = (
        jax.ShapeDtypeStruct((n_seq, D_MODEL), F32),
        jax.ShapeDtypeStruct((n_seq, KV_WIDTH, WINDOW), F32),
        jax.ShapeDtypeStruct((n_seq, KV_WIDTH, WINDOW), F32),
        jax.ShapeDtypeStruct((POOL_STATE_ROWS, n_seq, POOL_WIDTH), F32),
    )
    scratch = [
        pltpu.VMEM((n_seq, D_MODEL), BF16),
        pltpu.VMEM((n_seq, POOL_WIDTH), F32),
        pltpu.VMEM((n_seq, POOL_WIDTH), F32),
        pltpu.VMEM((n_seq, N_Q_HEADS * LANES), F32),
        pltpu.VMEM((n_seq, KV_WIDTH), F32),
        pltpu.VMEM((n_seq, KV_WIDTH), F32),
        pltpu.VMEM((n_seq, ATTN_WIDTH), F32),
        pltpu.VMEM((n_seq, D_MODEL), F32),
        pltpu.VMEM((n_seq, D_MODEL), F32),
        pltpu.VMEM((n_seq, POOL_WIDTH), BF16),
        pltpu.VMEM((n_seq, ATTN_WIDTH), BF16),
        pltpu.VMEM((BC * N_Q_HEADS, LANES), F32),
        pltpu.VMEM((BC * N_Q_HEADS, LANES), F32),
        pltpu.VMEM((BC * N_Q_HEADS, LANES), F32),
        pltpu.VMEM((BC * N_Q_HEADS, LANES), F32),
        pltpu.VMEM((SUBLANES, WINDOW), F32),
    ]
    cache = pl.BlockSpec((BC, KV_WIDTH, WINDOW), lambda i: (i, 0, 0))
    hist = pl.BlockSpec((POOL_STATE_ROWS, BC, POOL_WIDTH), lambda i: (0, i, 0))
    return pl.pallas_call(
        _sample_kernel,
        grid=(n_seq // BC,),
        in_specs=[_SMEM, _SMEM, _resident((n_seq, D_MODEL)), cache, cache, hist] + _weight_specs(),
        out_specs=(pl.BlockSpec((n_seq, D_MODEL), lambda i: (0, 0)), cache, cache, hist),
        out_shape=out_shape,
        scratch_shapes=scratch,
        compiler_params=pltpu.CompilerParams(
            dimension_semantics=("arbitrary",), vmem_limit_bytes=VMEM_LIMIT_BYTES),
        name="sample_layer",
    )(relb, sinks, x, cache_kt, cache_vt, state_t, *weights)


def _cache_as_kd_slot(cache):
    n = cache.shape[0]
    return cache.transpose(0, 2, 3, 1).reshape(n, KV_WIDTH, WINDOW)


def _cache_from_kd_slot(cache_t):
    n = cache_t.shape[0]
    return cache_t.reshape(n, N_KV_HEADS, HEAD_DIM, WINDOW).transpose(0, 3, 1, 2)


def kernel(x_prompt, x_sample, cache_k, cache_v, state_pool, rel_bias, g_norm, w_in, pool_w_grp, pool_scale, attn_sinks, w_br_pool, w_br_attn, w_out, g_final):
    depth = g_norm.shape[0]
    assert depth == 1 and x_sample.shape[1] == 1
    l = 0
    w_in_bf, w_grp_bf, w_brp_bf, w_bra_bf, w_out_bf = _cast_weights(w_in, pool_w_grp, w_br_pool, w_br_attn, w_out)
    weights = (g_norm[l].reshape(1, D_MODEL), w_in_bf, w_grp_bf, pool_scale[l].reshape(1, POOL_WIDTH),
               w_brp_bf, w_bra_bf, w_out_bf, g_final.reshape(1, D_MODEL))
    sinks = attn_sinks[l]
    n_seq = x_sample.shape[0]

    y_p, k_p, v_p, pool_p = _prompt_call(x_prompt, rel_bias, sinks, weights)
    y_s, k_s, v_s, pool_s = _sample_call(
        x_sample.reshape(n_seq, D_MODEL), _cache_as_kd_slot(cache_k[l]), _cache_as_kd_slot(cache_v[l]),
        state_pool[l].transpose(1, 0, 2), rel_bias, sinks, weights)
    return (
        y_p,
        y_s.reshape(n_seq, 1, D_MODEL),
        _cache_from_kd_slot(k_p)[None],
        _cache_from_kd_slot(v_p)[None],
        pool_p[None],
        _cache_from_kd_slot(k_s)[None],
        _cache_from_kd_slot(v_s)[None],
        pool_s.transpose(1, 0, 2)[None],
    )
```

```python
import functools
import math

import jax
import jax.numpy as jnp
from jax import lax
from jax.experimental import pallas as pl
from jax.experimental.pallas import tpu as pltpu

D_MODEL = 1024
POOL_WINDOWS = (2, 4, 8, 16)
POOL_GROUPS = len(POOL_WINDOWS)
POOL_WIDTH = D_MODEL // 2
POOL_GROUP_DIM = POOL_WIDTH // POOL_GROUPS
POOL_STATE_ROWS = max(POOL_WINDOWS) - 1
HEAD_DIM = 64
N_KV_HEADS = 2
ATTN_WIDTH = D_MODEL // 2
N_Q_HEADS = ATTN_WIDTH // HEAD_DIM
GQA_GROUP = N_Q_HEADS // N_KV_HEADS
KV_WIDTH = N_KV_HEADS * HEAD_DIM
WINDOW = 128
N_BUCKETS = 32
MAX_DISTANCE = 128
RMS_EPS = 1e-6
IN_COLS = 2 * POOL_WIDTH + 2 * ATTN_WIDTH + 2 * KV_WIDTH + 2 * D_MODEL

OFF_U = 0
OFF_ZP = OFF_U + POOL_WIDTH
OFF_Q = OFF_ZP + POOL_WIDTH
OFF_K = OFF_Q + ATTN_WIDTH
OFF_V = OFF_K + KV_WIDTH
OFF_ZA = OFF_V + KV_WIDTH
OFF_GP = OFF_ZA + ATTN_WIDTH
OFF_GA = OFF_GP + D_MODEL

LANES = 128
SUBLANES = 8
POOL_HDR = 16
TM = 512
BC = 32
CAST_STEPS = 8
VMEM_LIMIT_BYTES = 56 * 1024 * 1024

assert KV_WIDTH == LANES and 2 * HEAD_DIM == LANES and GQA_GROUP % 2 == 0

F32 = jnp.float32
BF16 = jnp.bfloat16
NEG_INF = float("-inf")


def _sigmoid(x):
    return 1.0 / (1.0 + jnp.exp(-x))


def _silu(x):
    return x * _sigmoid(x)


def _rmsnorm(x, g):
    return x * lax.rsqrt(jnp.mean(x * x, axis=-1, keepdims=True) + RMS_EPS) * g


def _dot(a, b):
    return jnp.dot(a, b, preferred_element_type=F32)


def _t5_bucket(rel):
    n = jnp.maximum(rel, 0)
    max_exact = N_BUCKETS // 2
    nf = jnp.maximum(n, 1).astype(F32)
    v = jnp.log(nf / max_exact) / math.log(MAX_DISTANCE / max_exact) * (N_BUCKETS - max_exact)
    steps = jnp.zeros(rel.shape, jnp.int32)
    for k in range(1, N_BUCKETS - max_exact):
        steps = steps + jnp.where(v >= k, 1, 0)
    return jnp.where(n < max_exact, n, max_exact + steps)


def _bias_table(rel, relb_ref, h):
    bucket = _t5_bucket(rel)

    def body(b, tbl):
        return jnp.where(bucket == b, relb_ref[b, h], tbl)

    tbl = lax.fori_loop(0, N_BUCKETS, body, jnp.zeros(rel.shape, F32))
    return jnp.where((rel >= 0) & (rel < WINDOW), tbl, NEG_INF)


def _per_head_column(ref_scalar):
    head = lax.broadcasted_iota(jnp.int32, (N_Q_HEADS, 1), 0)
    col = jnp.zeros((N_Q_HEADS, 1), F32)
    for h in range(N_Q_HEADS):
        col = jnp.where(head == h, ref_scalar(h), col)
    return col


def _project(xn_ref, w_in_ref, off, width):
    return _dot(xn_ref[...], w_in_ref[:, off:off + width])


def _kv_lane_mask(shape):
    return lax.broadcasted_iota(jnp.int32, shape, len(shape) - 1) < HEAD_DIM


def _head_pairs():
    for grp in range(ATTN_WIDTH // LANES):
        kv, pair = divmod(grp, GQA_GROUP // 2)
        yield grp, kv, kv * GQA_GROUP + 2 * pair, kv * GQA_GROUP + 2 * pair + 1


def _head_blocks(q):
    low = _kv_lane_mask((q.shape[0], LANES))
    blocks = [None] * N_Q_HEADS
    for grp, kv, even, odd in _head_pairs():
        a = q[:, grp * LANES:(grp + 1) * LANES]
        swapped = pltpu.roll(a, HEAD_DIM, 1)
        if kv == 0:
            blocks[even], blocks[odd] = jnp.where(low, a, 0.0), jnp.where(low, swapped, 0.0)
        else:
            blocks[even], blocks[odd] = jnp.where(low, 0.0, swapped), jnp.where(low, 0.0, a)
    return blocks


def _gather_heads(outs):
    low = _kv_lane_mask(outs[0].shape)
    groups = []
    for _, kv, even, odd in _head_pairs():
        if kv == 0:
            groups.append(jnp.where(low, outs[even], pltpu.roll(outs[odd], HEAD_DIM, 1)))
        else:
            groups.append(jnp.where(low, pltpu.roll(outs[even], HEAD_DIM, 1), outs[odd]))
    return jnp.concatenate(groups, axis=-1)


def _pool_windows(x_ext, inv_cnt):
    outs = []
    for g, w in enumerate(POOL_WINDOWS):
        xg = x_ext[:, g * POOL_GROUP_DIM:(g + 1) * POOL_GROUP_DIM]
        acc = xg
        shift = 1
        while shift < w:
            acc = acc + pltpu.roll(acc, shift, 0)
            shift *= 2
        outs.append(acc[POOL_HDR:] * inv_cnt[g] - xg[POOL_HDR:])
    return jnp.concatenate(outs, axis=-1)


def _residual_sum(x, pooled, zp_act, att_act, sig_gp, sig_ga, w_grp_ref, pool_scale, w_brp_ref, w_bra_ref,
                  w_out_ref):
    pg = jnp.concatenate(
        [_dot(pooled[:, g * POOL_GROUP_DIM:(g + 1) * POOL_GROUP_DIM], w_grp_ref[g])
         for g in range(POOL_GROUPS)], axis=-1)
    br_pool = _dot((pg * pool_scale * zp_act).astype(BF16), w_brp_ref[...])
    br_attn = _dot(att_act, w_bra_ref[...])
    merged = sig_gp * br_pool + sig_ga * br_attn
    return x + _dot(merged.astype(BF16), w_out_ref[...])


def _cast_kernel(w_in_ref, w_grp_ref, w_brp_ref, w_bra_ref, w_out_ref,
                 o_in_ref, o_grp_ref, o_brp_ref, o_bra_ref, o_out_ref):
    o_in_ref[...] = w_in_ref[0].astype(BF16)
    o_grp_ref[...] = w_grp_ref[0].astype(BF16)
    o_brp_ref[...] = w_brp_ref[0].astype(BF16)
    o_bra_ref[...] = w_bra_ref[0].astype(BF16)
    o_out_ref[...] = w_out_ref[0].astype(BF16)


def _cast_weights(w_in, w_grp, w_brp, w_bra, w_out):
    def rows(n):
        assert n % (CAST_STEPS * 2 * SUBLANES) == 0
        return n // CAST_STEPS

    r_in, r_grp, r_br, r_out = rows(D_MODEL), rows(POOL_GROUP_DIM), rows(POOL_WIDTH), rows(D_MODEL)
    in_specs = [
        pl.BlockSpec((1, r_in, IN_COLS), lambda s: (0, s, 0)),
        pl.BlockSpec((1, POOL_GROUPS, r_grp, POOL_GROUP_DIM), lambda s: (0, 0, s, 0)),
        pl.BlockSpec((1, r_br, D_MODEL), lambda s: (0, s, 0)),
        pl.BlockSpec((1, r_br, D_MODEL), lambda s: (0, s, 0)),
        pl.BlockSpec((1, r_out, D_MODEL), lambda s: (0, s, 0)),
    ]
    out_specs = (
        pl.BlockSpec((r_in, IN_COLS), lambda s: (s, 0)),
        pl.BlockSpec((POOL_GROUPS, r_grp, POOL_GROUP_DIM), lambda s: (0, s, 0)),
        pl.BlockSpec((r_br, D_MODEL), lambda s: (s, 0)),
        pl.BlockSpec((r_br, D_MODEL), lambda s: (s, 0)),
        pl.BlockSpec((r_out, D_MODEL), lambda s: (s, 0)),
    )
    out_shape = (
        jax.ShapeDtypeStruct((D_MODEL, IN_COLS), BF16),
        jax.ShapeDtypeStruct((POOL_GROUPS, POOL_GROUP_DIM, POOL_GROUP_DIM), BF16),
        jax.ShapeDtypeStruct((POOL_WIDTH, D_MODEL), BF16),
        jax.ShapeDtypeStruct((ATTN_WIDTH, D_MODEL), BF16),
        jax.ShapeDtypeStruct((D_MODEL, D_MODEL), BF16),
    )
    return pl.pallas_call(
        _cast_kernel, grid=(CAST_STEPS,), in_specs=in_specs, out_specs=out_specs, out_shape=out_shape,
        compiler_params=pltpu.CompilerParams(dimension_semantics=("arbitrary",)),
        name="cast_weights",
    )(w_in, w_grp, w_brp, w_bra, w_out)


def _prompt_kernel(n_tiles, tiles_per_seq,
                   relb_ref, sinks_ref, x_ref, xnext_ref, gnorm_ref, w_in_ref, w_grp_ref, pscale_ref,
                   w_brp_ref, w_bra_ref, w_out_ref, gfinal_ref,
                   y_ref, knew_ref, vnew_ref, pnew_ref,
                   xn_s, xnext_s, res_s, uext_s, zp_s, qh_s, kband_s, vband_s, za_s, gp_s, ga_s, pooled_s,
                   att_s, bias_s):
    s = pl.program_id(0)
    j = s % tiles_per_seq
    nblk = TM // WINDOW
    hrows = N_Q_HEADS * WINDOW

    @pl.when(s == 0)
    def _first_step():
        qi = lax.broadcasted_iota(jnp.int32, (WINDOW, 2 * WINDOW), 0)
        kc = lax.broadcasted_iota(jnp.int32, (WINDOW, 2 * WINDOW), 1)
        for h in range(N_Q_HEADS):
            bias_s[h * WINDOW:(h + 1) * WINDOW, :] = _bias_table(qi - kc + WINDOW, relb_ref, h)
        xn_s[...] = _rmsnorm(x_ref[0], gnorm_ref[...]).astype(BF16)
        res_s[...] = jnp.zeros((TM, D_MODEL), F32)

    @pl.when((s < n_tiles) & (j == 0))
    def _reset_carry():
        uext_s[0:POOL_HDR, :] = jnp.zeros((POOL_HDR, POOL_WIDTH), F32)
        kband_s[0:WINDOW, :] = jnp.zeros((WINDOW, KV_WIDTH), BF16)
        vband_s[0:WINDOW, :] = jnp.zeros((WINDOW, KV_WIDTH), BF16)

    @pl.when(s == n_tiles)
    def _last_norm():
        y_ref[0] = _rmsnorm(res_s[...], gfinal_ref[...])

    @pl.when(s < n_tiles)
    def _tile():
        uext_s[POOL_HDR:, :] = _project(xn_s, w_in_ref, OFF_U, POOL_WIDTH)
        y_ref[0] = _rmsnorm(res_s[...], gfinal_ref[...])
        q_heads = _head_blocks(_project(xn_s, w_in_ref, OFF_Q, ATTN_WIDTH) * (HEAD_DIM ** -0.5))
        for c in range(nblk):
            for h in range(N_Q_HEADS):
                r0 = c * hrows + h * WINDOW
                qh_s[r0:r0 + WINDOW, :] = q_heads[h][c * WINDOW:(c + 1) * WINDOW].astype(BF16)

        pos = j * TM + lax.broadcasted_iota(jnp.int32, (TM, 1), 0)
        inv_cnt = [1.0 / jnp.minimum(pos + 1, w).astype(F32) for w in POOL_WINDOWS]
        x_ext = uext_s[...]
        pooled_s[...] = _pool_windows(x_ext, inv_cnt).astype(BF16)
        pnew_ref[0] = x_ext[POOL_HDR + TM - POOL_STATE_ROWS:, :]
        uext_s[0:POOL_HDR, :] = x_ext[TM:, :]

        kv = _project(xn_s, w_in_ref, OFF_K, 2 * KV_WIDTH)
        kband_s[WINDOW:, :] = kv[:, :KV_WIDTH].astype(BF16)
        vband_s[WINDOW:, :] = kv[:, KV_WIDTH:].astype(BF16)
        knew_ref[0] = kv[TM - WINDOW:, :KV_WIDTH].T
        vnew_ref[0] = kv[TM - WINDOW:, KV_WIDTH:].T
        xnext_s[...] = _rmsnorm(xnext_ref[0], gnorm_ref[...]).astype(BF16)
        za_s[...] = _silu(_project(xn_s, w_in_ref, OFF_ZA, ATTN_WIDTH))

        first_neg = jnp.where(j == 0, NEG_INF, 0.0).astype(F32)
        for c in range(nblk):
            rows = slice(c * WINDOW, (c + 1) * WINDOW)
            k_c = kband_s[c * WINDOW:(c + 2) * WINDOW, :]
            v_c = vband_s[c * WINDOW:(c + 2) * WINDOW, :]
            s_all = lax.dot_general(qh_s[c * hrows:(c + 1) * hrows, :], k_c, (((1,), (1,)), ((), ())),
                                    preferred_element_type=F32)
            p_parts, inv_parts = [], []
            for h in range(N_Q_HEADS):
                head_rows = slice(h * WINDOW, (h + 1) * WINDOW)
                sc = s_all[head_rows, :] + bias_s[head_rows, :]
                if c == 0:
                    sc = jnp.concatenate([sc[:, :WINDOW] + first_neg, sc[:, WINDOW:]], axis=-1)
                sink = sinks_ref[h]
                m = jnp.maximum(jnp.max(sc, axis=-1, keepdims=True), sink)
                p = jnp.exp(sc - m)
                denom = jnp.sum(p, axis=-1, keepdims=True) + jnp.exp(sink - m)
                p_parts.append(p.astype(BF16))
                inv_parts.append(1.0 / denom)
            o_all = _dot(jnp.concatenate(p_parts, axis=0), v_c)
            outs = [o_all[h * WINDOW:(h + 1) * WINDOW, :] * inv_parts[h] for h in range(N_Q_HEADS)]
            att_s[rows, :] = (_gather_heads(outs) * za_s[rows, :]).astype(BF16)
        kband_s[0:WINDOW, :] = kband_s[TM:, :]
        vband_s[0:WINDOW, :] = vband_s[TM:, :]

        zp_s[...] = _silu(_project(xn_s, w_in_ref, OFF_ZP, POOL_WIDTH))
        for half in range(2):
            cols = slice(half * POOL_WIDTH, (half + 1) * POOL_WIDTH)
            gp_s[:, cols] = _sigmoid(_project(xn_s, w_in_ref, OFF_GP + half * POOL_WIDTH, POOL_WIDTH))
            ga_s[:, cols] = _sigmoid(_project(xn_s, w_in_ref, OFF_GA + half * POOL_WIDTH, POOL_WIDTH))

        res_s[...] = _residual_sum(x_ref[0], pooled_s[...], zp_s[...], att_s[...], gp_s[...], ga_s[...],
                                   w_grp_ref, pscale_ref[...], w_brp_ref, w_bra_ref, w_out_ref)
        xn_s[...] = xnext_s[...]


def _resident(shape):
    return pl.BlockSpec(shape, lambda *_: (0,) * len(shape), pipeline_mode=pl.Buffered(1))


_SMEM = pl.BlockSpec(memory_space=pltpu.SMEM)


def _weight_specs():
    return [
        _resident((1, D_MODEL)),
        _resident((D_MODEL, IN_COLS)),
        _resident((POOL_GROUPS, POOL_GROUP_DIM, POOL_GROUP_DIM)),
        _resident((1, POOL_WIDTH)),
        _resident((POOL_WIDTH, D_MODEL)),
        _resident((ATTN_WIDTH, D_MODEL)),
        _resident((D_MODEL, D_MODEL)),
        _resident((1, D_MODEL)),
    ]


def _prompt_call(x, relb, sinks, weights):
    batch, seq, _ = x.shape
    assert seq % TM == 0 and TM % WINDOW == 0 and TM >= POOL_HDR
    tiles_per_seq = seq // TM
    n_tiles = batch * tiles_per_seq

    def tile_block(tile_of_step):
        def index_map(s):
            t = jnp.clip(tile_of_step(s), 0, n_tiles - 1)
            return (t // tiles_per_seq, t % tiles_per_seq, 0)
        return pl.BlockSpec((1, TM, D_MODEL), index_map)

    def seq_block(shape):
        return pl.BlockSpec((1,) + shape, lambda s: (jnp.minimum(s, n_tiles - 1) // tiles_per_seq, 0, 0))

    out_shape = (
        jax.ShapeDtypeStruct((batch, seq, D_MODEL), F32),
        jax.ShapeDtypeStruct((batch, KV_WIDTH, WINDOW), F32),
        jax.ShapeDtypeStruct((batch, KV_WIDTH, WINDOW), F32),
        jax.ShapeDtypeStruct((batch, POOL_STATE_ROWS, POOL_WIDTH), F32),
    )
    scratch = [
        pltpu.VMEM((TM, D_MODEL), BF16),
        pltpu.VMEM((TM, D_MODEL), BF16),
        pltpu.VMEM((TM, D_MODEL), F32),
        pltpu.VMEM((POOL_HDR + TM, POOL_WIDTH), F32),
        pltpu.VMEM((TM, POOL_WIDTH), F32),
        pltpu.VMEM((TM * N_Q_HEADS, LANES), BF16),
        pltpu.VMEM((WINDOW + TM, KV_WIDTH), BF16),
        pltpu.VMEM((WINDOW + TM, KV_WIDTH), BF16),
        pltpu.VMEM((TM, ATTN_WIDTH), F32),
        pltpu.VMEM((TM, D_MODEL), F32),
        pltpu.VMEM((TM, D_MODEL), F32),
        pltpu.VMEM((TM, POOL_WIDTH), BF16),
        pltpu.VMEM((TM, ATTN_WIDTH), BF16),
        pltpu.VMEM((N_Q_HEADS * WINDOW, 2 * WINDOW), F32),
    ]
    return pl.pallas_call(
        functools.partial(_prompt_kernel, n_tiles, tiles_per_seq),
        grid=(n_tiles + 1,),
        in_specs=[_SMEM, _SMEM, tile_block(lambda s: s), tile_block(lambda s: s + 1)] + _weight_specs(),
        out_specs=(
            tile_block(lambda s: s - 1),
            seq_block((KV_WIDTH, WINDOW)),
            seq_block((KV_WIDTH, WINDOW)),
            seq_block((POOL_STATE_ROWS, POOL_WIDTH)),
        ),
        out_shape=out_shape,
        scratch_shapes=scratch,
        compiler_params=pltpu.CompilerParams(
            dimension_semantics=("arbitrary",), vmem_limit_bytes=VMEM_LIMIT_BYTES),
        name="prompt_layer",
    )(relb, sinks, x, x, *weights)


def _sample_kernel(relb_ref, sinks_ref, x_ref, kt_ref, vt_ref, st_ref,
                   gnorm_ref, w_in_ref, w_grp_ref, pscale_ref, w_brp_ref, w_bra_ref, w_out_ref, gfinal_ref,
                   y_ref, knew_ref, vnew_ref, pnew_ref,
                   xn_s, u_s, zp_s, qh_s, k_s, v_s, za_s, gp_s, ga_s, pooled_s, att_s,
                   q2_s, kn2_s, vn2_s, o2_s, bias_s):
    i = pl.program_id(0)
    n_steps = pl.num_programs(0)
    chunk = pl.ds(pl.multiple_of(i * BC, BC), BC)

    @pl.when(i == 0)
    def _project_all():
        slot = lax.broadcasted_iota(jnp.int32, (SUBLANES, WINDOW), 1)
        head = lax.broadcasted_iota(jnp.int32, (SUBLANES, WINDOW), 0)
        tbl = jnp.zeros((SUBLANES, WINDOW), F32)
        for h in range(N_Q_HEADS):
            tbl = jnp.where(head == h, _bias_table(WINDOW - slot, relb_ref, h), tbl)
        bias_s[...] = tbl

        xn_s[...] = _rmsnorm(x_ref[...], gnorm_ref[...]).astype(BF16)
        u_s[...] = _project(xn_s, w_in_ref, OFF_U, POOL_WIDTH)
        zp_s[...] = _silu(_project(xn_s, w_in_ref, OFF_ZP, POOL_WIDTH))
        q_heads = _head_blocks(_project(xn_s, w_in_ref, OFF_Q, ATTN_WIDTH) * (HEAD_DIM ** -0.5))
        for h in range(N_Q_HEADS):
            qh_s[:, h * LANES:(h + 1) * LANES] = q_heads[h]
        kv = _project(xn_s, w_in_ref, OFF_K, 2 * KV_WIDTH)
        k_s[...] = kv[:, :KV_WIDTH]
        v_s[...] = kv[:, KV_WIDTH:]
        za_s[...] = _silu(_project(xn_s, w_in_ref, OFF_ZA, ATTN_WIDTH))
        for half in range(2):
            cols = slice(half * POOL_WIDTH, (half + 1) * POOL_WIDTH)
            gp_s[:, cols] = _sigmoid(_project(xn_s, w_in_ref, OFF_GP + half * POOL_WIDTH, POOL_WIDTH))
            ga_s[:, cols] = _sigmoid(_project(xn_s, w_in_ref, OFF_GA + half * POOL_WIDTH, POOL_WIDTH))

    u_new = u_s[chunk, :]
    n_hist = POOL_STATE_ROWS
    parts = []
    for g, w in enumerate(POOL_WINDOWS):
        cols = slice(g * POOL_GROUP_DIM, (g + 1) * POOL_GROUP_DIM)
        acc = u_new[:, cols]
        for r in range(n_hist - (w - 1), n_hist):
            acc = acc + st_ref[r, :, cols]
        parts.append(acc * (1.0 / w) - u_new[:, cols])
    pooled_s[chunk, :] = jnp.concatenate(parts, axis=-1).astype(BF16)
    pnew_ref[0:n_hist - 1] = st_ref[1:n_hist]
    pnew_ref[n_hist - 1] = u_new

    k_new = k_s[chunk, :]
    v_new = v_s[chunk, :]
    last_slot = lax.broadcasted_iota(jnp.int32, (KV_WIDTH, WINDOW), 1) == WINDOW - 1
    k_new_t = k_new.T
    v_new_t = v_new.T
    for s in range(BC):
        knew_ref[s] = jnp.where(last_slot, k_new_t[:, s:s + 1], pltpu.roll(kt_ref[s], WINDOW - 1, 1))
        vnew_ref[s] = jnp.where(last_slot, v_new_t[:, s:s + 1], pltpu.roll(vt_ref[s], WINDOW - 1, 1))

    for h in range(N_Q_HEADS):
        q2_s[pl.ds(h, BC, stride=N_Q_HEADS), :] = qh_s[chunk, h * LANES:(h + 1) * LANES]
        kn2_s[pl.ds(h, BC, stride=N_Q_HEADS), :] = k_new
        vn2_s[pl.ds(h, BC, stride=N_Q_HEADS), :] = v_new
    q3 = q2_s[...].reshape(BC, N_Q_HEADS, LANES)
    kn3 = kn2_s[...].reshape(BC, N_Q_HEADS, LANES)
    vn3 = vn2_s[...].reshape(BC, N_Q_HEADS, LANES)
    s3 = lax.dot_general(q3.astype(BF16), kt_ref[...].astype(BF16), (((2,), (1,)), ((0,), (0,))),
                         preferred_element_type=F32) + bias_s[...][None]
    s_self = jnp.sum(q3 * kn3, axis=-1, keepdims=True) + _per_head_column(lambda h: relb_ref[0, h])[None]
    sink = _per_head_column(lambda h: sinks_ref[h])[None]
    m = jnp.maximum(jnp.maximum(jnp.max(s3, axis=-1, keepdims=True), s_self), sink)
    p3 = jnp.exp(s3 - m)
    p_self = jnp.exp(s_self - m)
    denom = jnp.sum(p3, axis=-1, keepdims=True) + p_self + jnp.exp(sink - m)
    o3 = lax.dot_general(p3.astype(BF16), vt_ref[...].astype(BF16), (((2,), (2,)), ((0,), (0,))),
                         preferred_element_type=F32)
    o3 = (o3 + p_self * vn3) * (1.0 / denom)
    o2_s[...] = o3.reshape(BC * N_Q_HEADS, LANES)
    outs = [o2_s[pl.ds(h, BC, stride=N_Q_HEADS), :] for h in range(N_Q_HEADS)]
    att_s[chunk, :] = (_gather_heads(outs) * za_s[chunk, :]).astype(BF16)

    @pl.when(i == n_steps - 1)
    def _merge_all():
        out = _residual_sum(x_ref[...], pooled_s[...], zp_s[...], att_s[...], gp_s[...], ga_s[...],
                            w_grp_ref, pscale_ref[...], w_brp_ref, w_bra_ref, w_out_ref)
        y_ref[...] = _rmsnorm(out, gfinal_ref[...])


def _sample_call(x, cache_kt, cache_vt, state_t, relb, sinks, weights):
    n_seq = x.shape[0]
    assert n_seq % BC == 0
    out_shape = (
        jax.ShapeDtypeStruct((n_seq, D_MODEL), F32),
        jax.ShapeDtypeStruct((n_seq, KV_WIDTH, WINDOW), F32),
        jax.ShapeDtypeStruct((n_seq, KV_WIDTH, WINDOW), F32),
        jax.ShapeDtypeStruct((POOL_STATE_ROWS, n_seq, POOL_WIDTH), F32),
    )
    scratch = [
        pltpu.VMEM((n_seq, D_MODEL), BF16),
        pltpu.VMEM((n_seq, POOL_WIDTH), F32),
        pltpu.VMEM((n_seq, POOL_WIDTH), F32),
        pltpu.VMEM((n_seq, N_Q_HEADS * LANES), F32),
        pltpu.VMEM((n_seq, KV_WIDTH), F32),
        pltpu.VMEM((n_seq, KV_WIDTH), F32),
        pltpu.VMEM((n_seq, ATTN_WIDTH), F32),
        pltpu.VMEM((n_seq, D_MODEL), F32),
        pltpu.VMEM((n_seq, D_MODEL), F32),
        pltpu.VMEM((n_seq, POOL_WIDTH), BF16),
        pltpu.VMEM((n_seq, ATTN_WIDTH), BF16),
        pltpu.VMEM((BC * N_Q_HEADS, LANES), F32),
        pltpu.VMEM((BC * N_Q_HEADS, LANES), F32),
        pltpu.VMEM((BC * N_Q_HEADS, LANES), F32),
        pltpu.VMEM((BC * N_Q_HEADS, LANES), F32),
        pltpu.VMEM((SUBLANES, WINDOW), F32),
    ]
    cache = pl.BlockSpec((BC, KV_WIDTH, WINDOW), lambda i: (i, 0, 0))
    hist = pl.BlockSpec((POOL_STATE_ROWS, BC, POOL_WIDTH), lambda i: (0, i, 0))
    return pl.pallas_call(
        _sample_kernel,
        grid=(n_seq // BC,),
        in_specs=[_SMEM, _SMEM, _resident((n_seq, D_MODEL)), cache, cache, hist] + _weight_specs(),
        out_specs=(pl.BlockSpec((n_seq, D_MODEL), lambda i: (0, 0)), cache, cache, hist),
        out_shape=out_shape,
        scratch_shapes=scratch,
        compiler_params=pltpu.CompilerParams(
            dimension_semantics=("arbitrary",), vmem_limit_bytes=VMEM_LIMIT_BYTES),
        name="sample_layer",
    )(relb, sinks, x, cache_kt, cache_vt, state_t, *weights)


def _cache_as_kd_slot(cache):
    n = cache.shape[0]
    return cache.transpose(0, 2, 3, 1).reshape(n, KV_WIDTH, WINDOW)


def _cache_from_kd_slot(cache_t):
    n = cache_t.shape[0]
    return cache_t.reshape(n, N_KV_HEADS, HEAD_DIM, WINDOW).transpose(0, 3, 1, 2)


def kernel(x_prompt, x_sample, cache_k, cache_v, state_pool, rel_bias, g_norm, w_in, pool_w_grp, pool_scale, attn_sinks, w_br_pool, w_br_attn, w_out, g_final):
    depth = g_norm.shape[0]
    assert depth == 1 and x_sample.shape[1] == 1
    l = 0
    w_in_bf, w_grp_bf, w_brp_bf, w_bra_bf, w_out_bf = _cast_weights(w_in, pool_w_grp, w_br_pool, w_br_attn, w_out)
    weights = (g_norm[l].reshape(1, D_MODEL), w_in_bf, w_grp_bf, pool_scale[l].reshape(1, POOL_WIDTH),
               w_brp_bf, w_bra_bf, w_out_bf, g_final.reshape(1, D_MODEL))
    sinks = attn_sinks[l]
    n_seq = x_sample.shape[0]

    y_p, k_p, v_p, pool_p = _prompt_call(x_prompt, rel_bias, sinks, weights)
    y_s, k_s, v_s, pool_s = _sample_call(
        x_sample.reshape(n_seq, D_MODEL), _cache_as_kd_slot(cache_k[l]), _cache_as_kd_slot(cache_v[l]),
        state_pool[l].transpose(1, 0, 2), rel_bias, sinks, weights)
    return (
        y_p,
        y_s.reshape(n_seq, 1, D_MODEL),
        _cache_from_kd_slot(k_p)[None],
        _cache_from_kd_slot(v_p)[None],
        pool_p[None],
        _cache_from_kd_slot(k_s)[None],
        _cache_from_kd_slot(v_s)[None],
        pool_s.transpose(1, 0, 2)[None],
    )
```

```python
import math

import jax
import jax.numpy as jnp
from jax import lax
from jax.experimental import pallas as pl
from jax.experimental.pallas import tpu as pltpu

D_MODEL = 1024
POOL_WINDOWS = (2, 4, 8, 16)
POOL_GROUPS = len(POOL_WINDOWS)
POOL_WIDTH = D_MODEL // 2
POOL_GROUP_DIM = POOL_WIDTH // POOL_GROUPS
POOL_STATE_ROWS = max(POOL_WINDOWS) - 1
HEAD_DIM = 64
N_KV_HEADS = 2
ATTN_WIDTH = D_MODEL // 2
N_Q_HEADS = ATTN_WIDTH // HEAD_DIM
GQA_GROUP = N_Q_HEADS // N_KV_HEADS
KV_WIDTH = N_KV_HEADS * HEAD_DIM
WINDOW = 128
N_BUCKETS = 32
MAX_DISTANCE = 128
RMS_EPS = 1e-6
IN_COLS = 2 * POOL_WIDTH + 2 * ATTN_WIDTH + 2 * KV_WIDTH + 2 * D_MODEL

OFF_U = 0
OFF_ZP = OFF_U + POOL_WIDTH
OFF_Q = OFF_ZP + POOL_WIDTH
OFF_K = OFF_Q + ATTN_WIDTH
OFF_V = OFF_K + KV_WIDTH
OFF_ZA = OFF_V + KV_WIDTH
OFF_GP = OFF_ZA + ATTN_WIDTH
OFF_GA = OFF_GP + D_MODEL

LANES = 128
SUBLANES = 8
POOL_HDR = 16
TM = 512
BC = 32
CAST_STEPS = 8
VMEM_LIMIT_BYTES = 56 * 1024 * 1024

assert KV_WIDTH == LANES and 2 * HEAD_DIM == LANES and GQA_GROUP % 2 == 0

F32 = jnp.float32
BF16 = jnp.bfloat16
NEG_INF = float("-inf")


def _sigmoid(x):
    return 1.0 / (1.0 + jnp.exp(-x))


def _silu(x):
    return x * _sigmoid(x)


def _rmsnorm(x, g):
    return x * lax.rsqrt(jnp.mean(x * x, axis=-1, keepdims=True) + RMS_EPS) * g


def _dot(a, b):
    return jnp.dot(a, b, preferred_element_type=F32)


def _t5_bucket(rel):
    n = jnp.maximum(rel, 0)
    max_exact = N_BUCKETS // 2
    nf = jnp.maximum(n, 1).astype(F32)
    v = jnp.log(nf / max_exact) / math.log(MAX_DISTANCE / max_exact) * (N_BUCKETS - max_exact)
    steps = jnp.zeros(rel.shape, jnp.int32)
    for k in range(1, N_BUCKETS - max_exact):
        steps = steps + jnp.where(v >= k, 1, 0)
    return jnp.where(n < max_exact, n, max_exact + steps)


def _bias_table(rel, relb_ref, h):
    bucket = _t5_bucket(rel)

    def body(b, tbl):
        return jnp.where(bucket == b, relb_ref[h, b], tbl)

    tbl = lax.fori_loop(0, N_BUCKETS, body, jnp.zeros(rel.shape, F32))
    return jnp.where((rel >= 0) & (rel < WINDOW), tbl, NEG_INF)


def _per_head_column(ref_scalar):
    head = lax.broadcasted_iota(jnp.int32, (N_Q_HEADS, 1), 0)
    col = jnp.zeros((N_Q_HEADS, 1), F32)
    for h in range(N_Q_HEADS):
        col = jnp.where(head == h, ref_scalar(h), col)
    return col


def _project(xn_ref, w_in_ref, off, width):
    return _dot(xn_ref[...], w_in_ref[:, off:off + width])


def _kv_lane_mask(shape):
    return lax.broadcasted_iota(jnp.int32, shape, len(shape) - 1) < HEAD_DIM


def _head_pairs():
    for grp in range(ATTN_WIDTH // LANES):
        kv, pair = divmod(grp, GQA_GROUP // 2)
        yield grp, kv, kv * GQA_GROUP + 2 * pair, kv * GQA_GROUP + 2 * pair + 1


def _head_blocks(q):
    low = _kv_lane_mask((q.shape[0], LANES))
    blocks = [None] * N_Q_HEADS
    for grp, kv, even, odd in _head_pairs():
        a = q[:, grp * LANES:(grp + 1) * LANES]
        swapped = pltpu.roll(a, HEAD_DIM, 1)
        if kv == 0:
            blocks[even], blocks[odd] = jnp.where(low, a, 0.0), jnp.where(low, swapped, 0.0)
        else:
            blocks[even], blocks[odd] = jnp.where(low, 0.0, swapped), jnp.where(low, 0.0, a)
    return blocks


def _gather_heads(outs):
    low = _kv_lane_mask(outs[0].shape)
    groups = []
    for _, kv, even, odd in _head_pairs():
        if kv == 0:
            groups.append(jnp.where(low, outs[even], pltpu.roll(outs[odd], HEAD_DIM, 1)))
        else:
            groups.append(jnp.where(low, pltpu.roll(outs[even], HEAD_DIM, 1), outs[odd]))
    return jnp.concatenate(groups, axis=-1)


def _pool_windows(x_ext, inv_cnt):
    outs = []
    for g, w in enumerate(POOL_WINDOWS):
        xg = x_ext[:, g * POOL_GROUP_DIM:(g + 1) * POOL_GROUP_DIM]
        acc = xg
        shift = 1
        while shift < w:
            acc = acc + pltpu.roll(acc, shift, 0)
            shift *= 2
        outs.append(acc[POOL_HDR:] * inv_cnt[g] - xg[POOL_HDR:])
    return jnp.concatenate(outs, axis=-1)


def _residual_sum(x, pooled, zp_act, att_act, sig_gp, sig_ga, w_grp_ref, pool_scale, w_brp_ref, w_bra_ref,
                  w_out_ref):
    pg = jnp.concatenate(
        [_dot(pooled[:, g * POOL_GROUP_DIM:(g + 1) * POOL_GROUP_DIM], w_grp_ref[g])
         for g in range(POOL_GROUPS)], axis=-1)
    br_pool = _dot((pg * pool_scale * zp_act).astype(BF16), w_brp_ref[...])
    br_attn = _dot(att_act, w_bra_ref[...])
    merged = sig_gp * br_pool + sig_ga * br_attn
    return x + _dot(merged.astype(BF16), w_out_ref[...])


def _cast_kernel(w_in_ref, w_grp_ref, w_brp_ref, w_bra_ref, w_out_ref,
                 o_in_ref, o_grp_ref, o_brp_ref, o_bra_ref, o_out_ref):
    o_in_ref[...] = w_in_ref[0].astype(BF16)
    o_grp_ref[...] = w_grp_ref[0].astype(BF16)
    o_brp_ref[...] = w_brp_ref[0].astype(BF16)
    o_bra_ref[...] = w_bra_ref[0].astype(BF16)
    o_out_ref[...] = w_out_ref[0].astype(BF16)


def _cast_weights(w_in, w_grp, w_brp, w_bra, w_out):
    def rows(n):
        assert n % (CAST_STEPS * 2 * SUBLANES) == 0
        return n // CAST_STEPS

    r_in, r_grp, r_br, r_out = rows(D_MODEL), rows(POOL_GROUP_DIM), rows(POOL_WIDTH), rows(D_MODEL)
    in_specs = [
        pl.BlockSpec((1, r_in, IN_COLS), lambda s: (0, s, 0)),
        pl.BlockSpec((1, POOL_GROUPS, r_grp, POOL_GROUP_DIM), lambda s: (0, 0, s, 0)),
        pl.BlockSpec((1, r_br, D_MODEL), lambda s: (0, s, 0)),
        pl.BlockSpec((1, r_br, D_MODEL), lambda s: (0, s, 0)),
        pl.BlockSpec((1, r_out, D_MODEL), lambda s: (0, s, 0)),
    ]
    out_specs = (
        pl.BlockSpec((r_in, IN_COLS), lambda s: (s, 0)),
        pl.BlockSpec((POOL_GROUPS, r_grp, POOL_GROUP_DIM), lambda s: (0, s, 0)),
        pl.BlockSpec((r_br, D_MODEL), lambda s: (s, 0)),
        pl.BlockSpec((r_br, D_MODEL), lambda s: (s, 0)),
        pl.BlockSpec((r_out, D_MODEL), lambda s: (s, 0)),
    )
    out_shape = (
        jax.ShapeDtypeStruct((D_MODEL, IN_COLS), BF16),
        jax.ShapeDtypeStruct((POOL_GROUPS, POOL_GROUP_DIM, POOL_GROUP_DIM), BF16),
        jax.ShapeDtypeStruct((POOL_WIDTH, D_MODEL), BF16),
        jax.ShapeDtypeStruct((ATTN_WIDTH, D_MODEL), BF16),
        jax.ShapeDtypeStruct((D_MODEL, D_MODEL), BF16),
    )
    return pl.pallas_call(
        _cast_kernel, grid=(CAST_STEPS,), in_specs=in_specs, out_specs=out_specs, out_shape=out_shape,
        compiler_params=pltpu.CompilerParams(dimension_semantics=("arbitrary",)),
        name="cast_weights",
    )(w_in, w_grp, w_brp, w_bra, w_out)


def _prompt_kernel(relb_ref, sinks_ref, x_ref, gnorm_ref, w_in_ref, w_grp_ref, pscale_ref,
                   w_brp_ref, w_bra_ref, w_out_ref, gfinal_ref,
                   y_ref, knew_ref, vnew_ref, pnew_ref,
                   xn_s, uext_s, zp_s, qh_s, kband_s, vband_s, za_s, gp_s, ga_s, pooled_s, att_s, bias_s):
    b = pl.program_id(0)
    j = pl.program_id(1)
    nblk = TM // WINDOW
    hrows = N_Q_HEADS * WINDOW

    @pl.when((b == 0) & (j == 0))
    def _build_bias():
        qi = lax.broadcasted_iota(jnp.int32, (WINDOW, 2 * WINDOW), 0)
        kc = lax.broadcasted_iota(jnp.int32, (WINDOW, 2 * WINDOW), 1)
        for h in range(N_Q_HEADS):
            bias_s[h * WINDOW:(h + 1) * WINDOW, :] = _bias_table(qi - kc + WINDOW, relb_ref, h)

    @pl.when(j == 0)
    def _reset_carry():
        uext_s[0:POOL_HDR, :] = jnp.zeros((POOL_HDR, POOL_WIDTH), F32)
        kband_s[0:WINDOW, :] = jnp.zeros((WINDOW, KV_WIDTH), BF16)
        vband_s[0:WINDOW, :] = jnp.zeros((WINDOW, KV_WIDTH), BF16)

    def _tile():
        xn_s[...] = _rmsnorm(x_ref[0], gnorm_ref[...]).astype(BF16)
        uext_s[POOL_HDR:, :] = _project(xn_s, w_in_ref, OFF_U, POOL_WIDTH)
        q_heads = _head_blocks(_project(xn_s, w_in_ref, OFF_Q, ATTN_WIDTH) * (HEAD_DIM ** -0.5))
        for c in range(nblk):
            for h in range(N_Q_HEADS):
                r0 = c * hrows + h * WINDOW
                qh_s[r0:r0 + WINDOW, :] = q_heads[h][c * WINDOW:(c + 1) * WINDOW].astype(BF16)

        pos = j * TM + lax.broadcasted_iota(jnp.int32, (TM, 1), 0)
        inv_cnt = [1.0 / jnp.minimum(pos + 1, w).astype(F32) for w in POOL_WINDOWS]
        x_ext = uext_s[...]
        pooled_s[...] = _pool_windows(x_ext, inv_cnt).astype(BF16)
        pnew_ref[0] = x_ext[POOL_HDR + TM - POOL_STATE_ROWS:, :]
        uext_s[0:POOL_HDR, :] = x_ext[TM:, :]

        kv = _project(xn_s, w_in_ref, OFF_K, 2 * KV_WIDTH)
        kband_s[WINDOW:, :] = kv[:, :KV_WIDTH].astype(BF16)
        vband_s[WINDOW:, :] = kv[:, KV_WIDTH:].astype(BF16)
        knew_ref[0] = kv[TM - WINDOW:, :KV_WIDTH].T
        vnew_ref[0] = kv[TM - WINDOW:, KV_WIDTH:].T
        za_s[...] = _silu(_project(xn_s, w_in_ref, OFF_ZA, ATTN_WIDTH))

        first_neg = jnp.where(j == 0, NEG_INF, 0.0).astype(F32)
        for c in range(nblk):
            rows = slice(c * WINDOW, (c + 1) * WINDOW)
            k_c = kband_s[c * WINDOW:(c + 2) * WINDOW, :]
            v_c = vband_s[c * WINDOW:(c + 2) * WINDOW, :]
            s_all = lax.dot_general(qh_s[c * hrows:(c + 1) * hrows, :], k_c, (((1,), (1,)), ((), ())),
                                    preferred_element_type=F32)
            p_parts, inv_parts = [], []
            for h in range(N_Q_HEADS):
                head_rows = slice(h * WINDOW, (h + 1) * WINDOW)
                sc = s_all[head_rows, :] + bias_s[head_rows, :]
                if c == 0:
                    sc = jnp.concatenate([sc[:, :WINDOW] + first_neg, sc[:, WINDOW:]], axis=-1)
                sink = sinks_ref[h]
                m = jnp.maximum(jnp.max(sc, axis=-1, keepdims=True), sink)
                p = jnp.exp(sc - m)
                denom = jnp.sum(p, axis=-1, keepdims=True) + jnp.exp(sink - m)
                p_parts.append(p.astype(BF16))
                inv_parts.append(1.0 / denom)
            o_all = _dot(jnp.concatenate(p_parts, axis=0), v_c)
            outs = [o_all[h * WINDOW:(h + 1) * WINDOW, :] * inv_parts[h] for h in range(N_Q_HEADS)]
            att_s[rows, :] = (_gather_heads(outs) * za_s[rows, :]).astype(BF16)
        kband_s[0:WINDOW, :] = kband_s[TM:, :]
        vband_s[0:WINDOW, :] = vband_s[TM:, :]

        zp_s[...] = _silu(_project(xn_s, w_in_ref, OFF_ZP, POOL_WIDTH))
        for half in range(2):
            cols = slice(half * POOL_WIDTH, (half + 1) * POOL_WIDTH)
            gp_s[:, cols] = _sigmoid(_project(xn_s, w_in_ref, OFF_GP + half * POOL_WIDTH, POOL_WIDTH))
            ga_s[:, cols] = _sigmoid(_project(xn_s, w_in_ref, OFF_GA + half * POOL_WIDTH, POOL_WIDTH))

        out = _residual_sum(x_ref[0], pooled_s[...], zp_s[...], att_s[...], gp_s[...], ga_s[...],
                            w_grp_ref, pscale_ref[...], w_brp_ref, w_bra_ref, w_out_ref)
        y_ref[0] = _rmsnorm(out, gfinal_ref[...])

    _tile()


def _resident(shape):
    return pl.BlockSpec(shape, lambda *_: (0,) * len(shape), pipeline_mode=pl.Buffered(1))


_SMEM = pl.BlockSpec(memory_space=pltpu.SMEM)


def _weight_specs():
    return [
        _resident((1, D_MODEL)),
        _resident((D_MODEL, IN_COLS)),
        _resident((POOL_GROUPS, POOL_GROUP_DIM, POOL_GROUP_DIM)),
        _resident((1, POOL_WIDTH)),
        _resident((POOL_WIDTH, D_MODEL)),
        _resident((ATTN_WIDTH, D_MODEL)),
        _resident((D_MODEL, D_MODEL)),
        _resident((1, D_MODEL)),
    ]


def _prompt_call(x, relb, sinks, weights):
    batch, seq, _ = x.shape
    assert seq % TM == 0 and TM % WINDOW == 0 and TM >= POOL_HDR
    tile_block = pl.BlockSpec((1, TM, D_MODEL), lambda b, j: (b, j, 0))

    def seq_block(shape):
        return pl.BlockSpec((1,) + shape, lambda b, j: (b, 0, 0))

    out_shape = (
        jax.ShapeDtypeStruct((batch, seq, D_MODEL), F32),
        jax.ShapeDtypeStruct((batch, KV_WIDTH, WINDOW), F32),
        jax.ShapeDtypeStruct((batch, KV_WIDTH, WINDOW), F32),
        jax.ShapeDtypeStruct((batch, POOL_STATE_ROWS, POOL_WIDTH), F32),
    )
    scratch = [
        pltpu.VMEM((TM, D_MODEL), BF16),
        pltpu.VMEM((POOL_HDR + TM, POOL_WIDTH), F32),
        pltpu.VMEM((TM, POOL_WIDTH), F32),
        pltpu.VMEM((TM * N_Q_HEADS, LANES), BF16),
        pltpu.VMEM((WINDOW + TM, KV_WIDTH), BF16),
        pltpu.VMEM((WINDOW + TM, KV_WIDTH), BF16),
        pltpu.VMEM((TM, ATTN_WIDTH), F32),
        pltpu.VMEM((TM, D_MODEL), F32),
        pltpu.VMEM((TM, D_MODEL), F32),
        pltpu.VMEM((TM, POOL_WIDTH), BF16),
        pltpu.VMEM((TM, ATTN_WIDTH), BF16),
        pltpu.VMEM((N_Q_HEADS * WINDOW, 2 * WINDOW), F32),
    ]
    return pl.pallas_call(
        _prompt_kernel,
        grid=(batch, seq // TM),
        in_specs=[_SMEM, _SMEM, tile_block] + _weight_specs(),
        out_specs=(
            tile_block,
            seq_block((KV_WIDTH, WINDOW)),
            seq_block((KV_WIDTH, WINDOW)),
            seq_block((POOL_STATE_ROWS, POOL_WIDTH)),
        ),
        out_shape=out_shape,
        scratch_shapes=scratch,
        compiler_params=pltpu.CompilerParams(
            dimension_semantics=("arbitrary", "arbitrary"), vmem_limit_bytes=VMEM_LIMIT_BYTES),
        name="prompt_layer",
    )(relb, sinks, x, *weights)


def _sample_kernel(relb_ref, sinks_ref, x_ref, kt_ref, vt_ref, st_ref,
                   gnorm_ref, w_in_ref, w_grp_ref, pscale_ref, w_brp_ref, w_bra_ref, w_out_ref, gfinal_ref,
                   y_ref, knew_ref, vnew_ref, pnew_ref,
                   xn_s, u_s, zp_s, qh_s, k_s, v_s, za_s, gp_s, ga_s, pooled_s, att_s,
                   q2_s, kn2_s, vn2_s, o2_s, bias_s):
    i = pl.program_id(0)
    n_steps = pl.num_programs(0)
    chunk = pl.ds(pl.multiple_of(i * BC, BC), BC)

    @pl.when(i == 0)
    def _project_all():
        slot = lax.broadcasted_iota(jnp.int32, (SUBLANES, WINDOW), 1)
        head = lax.broadcasted_iota(jnp.int32, (SUBLANES, WINDOW), 0)
        tbl = jnp.zeros((SUBLANES, WINDOW), F32)
        for h in range(N_Q_HEADS):
            tbl = jnp.where(head == h, _bias_table(WINDOW - slot, relb_ref, h), tbl)
        bias_s[...] = tbl

        xn_s[...] = _rmsnorm(x_ref[:, 0, :], gnorm_ref[...]).astype(BF16)
        u_s[...] = _project(xn_s, w_in_ref, OFF_U, POOL_WIDTH)
        zp_s[...] = _silu(_project(xn_s, w_in_ref, OFF_ZP, POOL_WIDTH))
        q_heads = _head_blocks(_project(xn_s, w_in_ref, OFF_Q, ATTN_WIDTH) * (HEAD_DIM ** -0.5))
        for h in range(N_Q_HEADS):
            qh_s[:, h * LANES:(h + 1) * LANES] = q_heads[h]
        kv = _project(xn_s, w_in_ref, OFF_K, 2 * KV_WIDTH)
        k_s[...] = kv[:, :KV_WIDTH]
        v_s[...] = kv[:, KV_WIDTH:]
        za_s[...] = _silu(_project(xn_s, w_in_ref, OFF_ZA, ATTN_WIDTH))
        for half in range(2):
            cols = slice(half * POOL_WIDTH, (half + 1) * POOL_WIDTH)
            gp_s[:, cols] = _sigmoid(_project(xn_s, w_in_ref, OFF_GP + half * POOL_WIDTH, POOL_WIDTH))
            ga_s[:, cols] = _sigmoid(_project(xn_s, w_in_ref, OFF_GA + half * POOL_WIDTH, POOL_WIDTH))

    u_new = u_s[chunk, :]
    n_hist = POOL_STATE_ROWS
    parts = []
    for g, w in enumerate(POOL_WINDOWS):
        cols = slice(g * POOL_GROUP_DIM, (g + 1) * POOL_GROUP_DIM)
        acc = u_new[:, cols]
        for r in range(n_hist - (w - 1), n_hist):
            acc = acc + st_ref[r, :, cols]
        parts.append(acc * (1.0 / w) - u_new[:, cols])
    pooled_s[chunk, :] = jnp.concatenate(parts, axis=-1).astype(BF16)
    pnew_ref[0:n_hist - 1] = st_ref[1:n_hist]
    pnew_ref[n_hist - 1] = u_new

    k_new = k_s[chunk, :]
    v_new = v_s[chunk, :]
    last_slot = lax.broadcasted_iota(jnp.int32, (KV_WIDTH, WINDOW), 1) == WINDOW - 1
    k_new_t = k_new.T
    v_new_t = v_new.T
    for s in range(BC):
        knew_ref[s] = jnp.where(last_slot, k_new_t[:, s:s + 1], pltpu.roll(kt_ref[s], WINDOW - 1, 1))
        vnew_ref[s] = jnp.where(last_slot, v_new_t[:, s:s + 1], pltpu.roll(vt_ref[s], WINDOW - 1, 1))

    for h in range(N_Q_HEADS):
        q2_s[pl.ds(h, BC, stride=N_Q_HEADS), :] = qh_s[chunk, h * LANES:(h + 1) * LANES]
        kn2_s[pl.ds(h, BC, stride=N_Q_HEADS), :] = k_new
        vn2_s[pl.ds(h, BC, stride=N_Q_HEADS), :] = v_new
    q3 = q2_s[...].reshape(BC, N_Q_HEADS, LANES)
    kn3 = kn2_s[...].reshape(BC, N_Q_HEADS, LANES)
    vn3 = vn2_s[...].reshape(BC, N_Q_HEADS, LANES)
    s3 = lax.dot_general(q3.astype(BF16), kt_ref[...].astype(BF16), (((2,), (1,)), ((0,), (0,))),
                         preferred_element_type=F32) + bias_s[...][None]
    s_self = jnp.sum(q3 * kn3, axis=-1, keepdims=True) + _per_head_column(lambda h: relb_ref[h, 0])[None]
    sink = _per_head_column(lambda h: sinks_ref[h])[None]
    m = jnp.maximum(jnp.maximum(jnp.max(s3, axis=-1, keepdims=True), s_self), sink)
    p3 = jnp.exp(s3 - m)
    p_self = jnp.exp(s_self - m)
    denom = jnp.sum(p3, axis=-1, keepdims=True) + p_self + jnp.exp(sink - m)
    o3 = lax.dot_general(p3.astype(BF16), vt_ref[...].astype(BF16), (((2,), (2,)), ((0,), (0,))),
                         preferred_element_type=F32)
    o3 = (o3 + p_self * vn3) * (1.0 / denom)
    o2_s[...] = o3.reshape(BC * N_Q_HEADS, LANES)
    outs = [o2_s[pl.ds(h, BC, stride=N_Q_HEADS), :] for h in range(N_Q_HEADS)]
    att_s[chunk, :] = (_gather_heads(outs) * za_s[chunk, :]).astype(BF16)

    @pl.when(i == n_steps - 1)
    def _merge_all():
        out = _residual_sum(x_ref[:, 0, :], pooled_s[...], zp_s[...], att_s[...], gp_s[...], ga_s[...],
                            w_grp_ref, pscale_ref[...], w_brp_ref, w_bra_ref, w_out_ref)
        y_ref[:, 0, :] = _rmsnorm(out, gfinal_ref[...])


def _sample_call(x, cache_kt, cache_vt, state_t, relb, sinks, weights):
    n_seq = x.shape[0]
    assert n_seq % BC == 0
    out_shape = (
        jax.ShapeDtypeStruct((n_seq, 1, D_MODEL), F32),
        jax.ShapeDtypeStruct((n_seq, KV_WIDTH, WINDOW), F32),
        jax.ShapeDtypeStruct((n_seq, KV_WIDTH, WINDOW), F32),
        jax.ShapeDtypeStruct((POOL_STATE_ROWS, n_seq, POOL_WIDTH), F32),
    )
    scratch = [
        pltpu.VMEM((n_seq, D_MODEL), BF16),
        pltpu.VMEM((n_seq, POOL_WIDTH), F32),
        pltpu.VMEM((n_seq, POOL_WIDTH), F32),
        pltpu.VMEM((n_seq, N_Q_HEADS * LANES), F32),
        pltpu.VMEM((n_seq, KV_WIDTH), F32),
        pltpu.VMEM((n_seq, KV_WIDTH), F32),
        pltpu.VMEM((n_seq, ATTN_WIDTH), F32),
        pltpu.VMEM((n_seq, D_MODEL), F32),
        pltpu.VMEM((n_seq, D_MODEL), F32),
        pltpu.VMEM((n_seq, POOL_WIDTH), BF16),
        pltpu.VMEM((n_seq, ATTN_WIDTH), BF16),
        pltpu.VMEM((BC * N_Q_HEADS, LANES), F32),
        pltpu.VMEM((BC * N_Q_HEADS, LANES), F32),
        pltpu.VMEM((BC * N_Q_HEADS, LANES), F32),
        pltpu.VMEM((BC * N_Q_HEADS, LANES), F32),
        pltpu.VMEM((SUBLANES, WINDOW), F32),
    ]
    cache = pl.BlockSpec((BC, KV_WIDTH, WINDOW), lambda i: (i, 0, 0))
    hist = pl.BlockSpec((POOL_STATE_ROWS, BC, POOL_WIDTH), lambda i: (0, i, 0))
    return pl.pallas_call(
        _sample_kernel,
        grid=(n_seq // BC,),
        in_specs=[_SMEM, _SMEM, _resident((n_seq, 1, D_MODEL)), cache, cache, hist] + _weight_specs(),
        out_specs=(pl.BlockSpec((n_seq, 1, D_MODEL), lambda i: (0, 0, 0)), cache, cache, hist),
        out_shape=out_shape,
        scratch_shapes=scratch,
        compiler_params=pltpu.CompilerParams(
            dimension_semantics=("arbitrary",), vmem_limit_bytes=VMEM_LIMIT_BYTES),
        name="sample_layer",
    )(relb, sinks, x, cache_kt, cache_vt, state_t, *weights)


def _cache_as_kd_slot(cache):
    n = cache.shape[0]
    return cache.transpose(0, 2, 3, 1).reshape(n, KV_WIDTH, WINDOW)


def _cache_from_kd_slot(cache_t):
    n = cache_t.shape[0]
    return cache_t.reshape(n, N_KV_HEADS, HEAD_DIM, WINDOW).transpose(0, 3, 1, 2)


def kernel(x_prompt, x_sample, cache_k, cache_v, state_pool, rel_bias, g_norm, w_in, pool_w_grp, pool_scale, attn_sinks, w_br_pool, w_br_attn, w_out, g_final):
    depth = g_norm.shape[0]
    assert depth == 1 and x_sample.shape[1] == 1
    l = 0
    w_in_bf, w_grp_bf, w_brp_bf, w_bra_bf, w_out_bf = _cast_weights(w_in, pool_w_grp, w_br_pool, w_br_attn, w_out)
    weights = (g_norm[l].reshape(1, D_MODEL), w_in_bf, w_grp_bf, pool_scale[l].reshape(1, POOL_WIDTH),
               w_brp_bf, w_bra_bf, w_out_bf, g_final.reshape(1, D_MODEL))
    sinks = attn_sinks[l]
    relb = rel_bias.T

    y_p, k_p, v_p, pool_p = _prompt_call(x_prompt, relb, sinks, weights)
    y_s, k_s, v_s, pool_s = _sample_call(
        x_sample, _cache_as_kd_slot(cache_k[l]), _cache_as_kd_slot(cache_v[l]),
        state_pool[l].transpose(1, 0, 2), relb, sinks, weights)
    return (
        y_p,
        y_s,
        _cache_from_kd_slot(k_p)[None],
        _cache_from_kd_slot(v_p)[None],
        pool_p[None],
        _cache_from_kd_slot(k_s)[None],
        _cache_from_kd_slot(v_s)[None],
        pool_s.transpose(1, 0, 2)[None],
    )
```

```python
import math

import jax
import jax.numpy as jnp
from jax import lax
from jax.experimental import pallas as pl
from jax.experimental.pallas import tpu as pltpu

D_MODEL = 1024
POOL_WINDOWS = (2, 4, 8, 16)
POOL_GROUPS = len(POOL_WINDOWS)
POOL_WIDTH = D_MODEL // 2
POOL_GROUP_DIM = POOL_WIDTH // POOL_GROUPS
POOL_STATE_ROWS = max(POOL_WINDOWS) - 1
HEAD_DIM = 64
N_KV_HEADS = 2
ATTN_WIDTH = D_MODEL // 2
N_Q_HEADS = ATTN_WIDTH // HEAD_DIM
GQA_GROUP = N_Q_HEADS // N_KV_HEADS
KV_WIDTH = N_KV_HEADS * HEAD_DIM
WINDOW = 128
N_BUCKETS = 32
MAX_DISTANCE = 128
RMS_EPS = 1e-6
IN_COLS = 2 * POOL_WIDTH + 2 * ATTN_WIDTH + 2 * KV_WIDTH + 2 * D_MODEL

OFF_U = 0
OFF_ZP = OFF_U + POOL_WIDTH
OFF_Q = OFF_ZP + POOL_WIDTH
OFF_K = OFF_Q + ATTN_WIDTH
OFF_V = OFF_K + KV_WIDTH
OFF_ZA = OFF_V + KV_WIDTH
OFF_GP = OFF_ZA + ATTN_WIDTH
OFF_GA = OFF_GP + D_MODEL

LANES = 128
SUBLANES = 8
POOL_HDR = 16
TM = 512
BC = 32
W_STEPS = 8
VMEM_LIMIT_BYTES = 56 * 1024 * 1024

assert KV_WIDTH == LANES and 2 * HEAD_DIM == LANES and GQA_GROUP % 2 == 0

F32 = jnp.float32
BF16 = jnp.bfloat16
NEG_INF = float("-inf")


def _sigmoid(x):
    return 1.0 / (1.0 + jnp.exp(-x))


def _silu(x):
    return x * _sigmoid(x)


def _rmsnorm(x, g):
    return x * lax.rsqrt(jnp.mean(x * x, axis=-1, keepdims=True) + RMS_EPS) * g


def _dot(a, b):
    return jnp.dot(a, b, preferred_element_type=F32)


def _t5_bucket(rel):
    n = jnp.maximum(rel, 0)
    max_exact = N_BUCKETS // 2
    nf = jnp.maximum(n, 1).astype(F32)
    v = jnp.log(nf / max_exact) / math.log(MAX_DISTANCE / max_exact) * (N_BUCKETS - max_exact)
    steps = jnp.zeros(rel.shape, jnp.int32)
    for k in range(1, N_BUCKETS - max_exact):
        steps = steps + jnp.where(v >= k, 1, 0)
    return jnp.where(n < max_exact, n, max_exact + steps)


def _bias_table(rel, relb_ref, h):
    bucket = _t5_bucket(rel)

    def body(b, tbl):
        return jnp.where(bucket == b, relb_ref[h, b], tbl)

    tbl = lax.fori_loop(0, N_BUCKETS, body, jnp.zeros(rel.shape, F32))
    return jnp.where((rel >= 0) & (rel < WINDOW), tbl, NEG_INF)


def _per_head_column(ref_scalar):
    head = lax.broadcasted_iota(jnp.int32, (N_Q_HEADS, 1), 0)
    col = jnp.zeros((N_Q_HEADS, 1), F32)
    for h in range(N_Q_HEADS):
        col = jnp.where(head == h, ref_scalar(h), col)
    return col


def _project(xn_ref, w_in_ref, off, width):
    return _dot(xn_ref[...], w_in_ref[:, off:off + width])


def _kv_lane_mask(shape):
    return lax.broadcasted_iota(jnp.int32, shape, len(shape) - 1) < HEAD_DIM


def _head_pairs():
    for grp in range(ATTN_WIDTH // LANES):
        kv, pair = divmod(grp, GQA_GROUP // 2)
        yield grp, kv, kv * GQA_GROUP + 2 * pair, kv * GQA_GROUP + 2 * pair + 1


def _head_blocks(q):
    low = _kv_lane_mask((q.shape[0], LANES))
    blocks = [None] * N_Q_HEADS
    for grp, kv, even, odd in _head_pairs():
        a = q[:, grp * LANES:(grp + 1) * LANES]
        swapped = pltpu.roll(a, HEAD_DIM, 1)
        if kv == 0:
            blocks[even], blocks[odd] = jnp.where(low, a, 0.0), jnp.where(low, swapped, 0.0)
        else:
            blocks[even], blocks[odd] = jnp.where(low, 0.0, swapped), jnp.where(low, 0.0, a)
    return blocks


def _gather_heads(outs):
    low = _kv_lane_mask(outs[0].shape)
    groups = []
    for _, kv, even, odd in _head_pairs():
        if kv == 0:
            groups.append(jnp.where(low, outs[even], pltpu.roll(outs[odd], HEAD_DIM, 1)))
        else:
            groups.append(jnp.where(low, pltpu.roll(outs[even], HEAD_DIM, 1), outs[odd]))
    return jnp.concatenate(groups, axis=-1)


def _pool_windows(x_ext, inv_cnt):
    outs = []
    for g, w in enumerate(POOL_WINDOWS):
        xg = x_ext[:, g * POOL_GROUP_DIM:(g + 1) * POOL_GROUP_DIM]
        acc = xg
        shift = 1
        while shift < w:
            acc = acc + pltpu.roll(acc, shift, 0)
            shift *= 2
        outs.append(acc[POOL_HDR:] * inv_cnt[g] - xg[POOL_HDR:])
    return jnp.concatenate(outs, axis=-1)


def _residual_sum(x, pooled, zp_act, att_act, sig_gp, sig_ga, w_grp_ref, pool_scale, w_brp_ref, w_bra_ref,
                  w_out_ref):
    pg = jnp.concatenate(
        [_dot(pooled[:, g * POOL_GROUP_DIM:(g + 1) * POOL_GROUP_DIM], w_grp_ref[g])
         for g in range(POOL_GROUPS)], axis=-1)
    br_pool = _dot((pg * pool_scale * zp_act).astype(BF16), w_brp_ref[...])
    br_attn = _dot(att_act, w_bra_ref[...])
    merged = sig_gp * br_pool + sig_ga * br_attn
    return x + _dot(merged.astype(BF16), w_out_ref[...])


def _prompt_kernel(relb_ref, sinks_ref, x_ref, gnorm_ref, w_in_ref, w_grp_ref, pscale_ref,
                   w_brp_ref, w_bra_ref, w_out_ref, gfinal_ref,
                   y_ref, knew_ref, vnew_ref, pnew_ref,
                   xn_s, uext_s, zp_s, qh_s, kband_s, vband_s, za_s, gp_s, ga_s, pooled_s, att_s, bias_s):
    b = pl.program_id(0)
    j = pl.program_id(1)
    nblk = TM // WINDOW
    hrows = N_Q_HEADS * WINDOW

    @pl.when((b == 0) & (j == 0))
    def _build_bias():
        qi = lax.broadcasted_iota(jnp.int32, (WINDOW, 2 * WINDOW), 0)
        kc = lax.broadcasted_iota(jnp.int32, (WINDOW, 2 * WINDOW), 1)
        for h in range(N_Q_HEADS):
            bias_s[h * WINDOW:(h + 1) * WINDOW, :] = _bias_table(qi - kc + WINDOW, relb_ref, h)

    @pl.when(j == 0)
    def _reset_carry():
        uext_s[0:POOL_HDR, :] = jnp.zeros((POOL_HDR, POOL_WIDTH), F32)
        kband_s[0:WINDOW, :] = jnp.zeros((WINDOW, KV_WIDTH), BF16)
        vband_s[0:WINDOW, :] = jnp.zeros((WINDOW, KV_WIDTH), BF16)

    def _tile():
        xn_s[...] = _rmsnorm(x_ref[0], gnorm_ref[...]).astype(BF16)
        uext_s[POOL_HDR:, :] = _project(xn_s, w_in_ref, OFF_U, POOL_WIDTH)
        q_heads = _head_blocks(_project(xn_s, w_in_ref, OFF_Q, ATTN_WIDTH) * (HEAD_DIM ** -0.5))
        for c in range(nblk):
            for h in range(N_Q_HEADS):
                r0 = c * hrows + h * WINDOW
                qh_s[r0:r0 + WINDOW, :] = q_heads[h][c * WINDOW:(c + 1) * WINDOW].astype(BF16)

        pos = j * TM + lax.broadcasted_iota(jnp.int32, (TM, 1), 0)
        inv_cnt = [1.0 / jnp.minimum(pos + 1, w).astype(F32) for w in POOL_WINDOWS]
        x_ext = uext_s[...]
        pooled_s[...] = _pool_windows(x_ext, inv_cnt).astype(BF16)
        pnew_ref[0] = x_ext[POOL_HDR + TM - POOL_STATE_ROWS:, :]
        uext_s[0:POOL_HDR, :] = x_ext[TM:, :]

        kv = _project(xn_s, w_in_ref, OFF_K, 2 * KV_WIDTH)
        kband_s[WINDOW:, :] = kv[:, :KV_WIDTH].astype(BF16)
        vband_s[WINDOW:, :] = kv[:, KV_WIDTH:].astype(BF16)
        knew_ref[0] = kv[TM - WINDOW:, :KV_WIDTH].T
        vnew_ref[0] = kv[TM - WINDOW:, KV_WIDTH:].T
        za_s[...] = _silu(_project(xn_s, w_in_ref, OFF_ZA, ATTN_WIDTH))

        first_neg = jnp.where(j == 0, NEG_INF, 0.0).astype(F32)
        for c in range(nblk):
            rows = slice(c * WINDOW, (c + 1) * WINDOW)
            k_c = kband_s[c * WINDOW:(c + 2) * WINDOW, :]
            v_c = vband_s[c * WINDOW:(c + 2) * WINDOW, :]
            s_all = lax.dot_general(qh_s[c * hrows:(c + 1) * hrows, :], k_c, (((1,), (1,)), ((), ())),
                                    preferred_element_type=F32)
            p_parts, inv_parts = [], []
            for h in range(N_Q_HEADS):
                head_rows = slice(h * WINDOW, (h + 1) * WINDOW)
                sc = s_all[head_rows, :] + bias_s[head_rows, :]
                if c == 0:
                    sc = jnp.concatenate([sc[:, :WINDOW] + first_neg, sc[:, WINDOW:]], axis=-1)
                sink = sinks_ref[h]
                m = jnp.maximum(jnp.max(sc, axis=-1, keepdims=True), sink)
                p = jnp.exp(sc - m)
                denom = jnp.sum(p, axis=-1, keepdims=True) + jnp.exp(sink - m)
                p_parts.append(p.astype(BF16))
                inv_parts.append(1.0 / denom)
            o_all = _dot(jnp.concatenate(p_parts, axis=0), v_c)
            outs = [o_all[h * WINDOW:(h + 1) * WINDOW, :] * inv_parts[h] for h in range(N_Q_HEADS)]
            att_s[rows, :] = (_gather_heads(outs) * za_s[rows, :]).astype(BF16)
        kband_s[0:WINDOW, :] = kband_s[TM:, :]
        vband_s[0:WINDOW, :] = vband_s[TM:, :]

        zp_s[...] = _silu(_project(xn_s, w_in_ref, OFF_ZP, POOL_WIDTH))
        for half in range(2):
            cols = slice(half * POOL_WIDTH, (half + 1) * POOL_WIDTH)
            gp_s[:, cols] = _sigmoid(_project(xn_s, w_in_ref, OFF_GP + half * POOL_WIDTH, POOL_WIDTH))
            ga_s[:, cols] = _sigmoid(_project(xn_s, w_in_ref, OFF_GA + half * POOL_WIDTH, POOL_WIDTH))

        out = _residual_sum(x_ref[0], pooled_s[...], zp_s[...], att_s[...], gp_s[...], ga_s[...],
                            w_grp_ref, pscale_ref[...], w_brp_ref, w_bra_ref, w_out_ref)
        y_ref[0] = _rmsnorm(out, gfinal_ref[...])

    _tile()


def _resident(shape):
    return pl.BlockSpec(shape, lambda *_: (0,) * len(shape), pipeline_mode=pl.Buffered(1))


_SMEM = pl.BlockSpec(memory_space=pltpu.SMEM)


def _weight_specs():
    return [
        _resident((1, D_MODEL)),
        _resident((D_MODEL, IN_COLS)),
        _resident((POOL_GROUPS, POOL_GROUP_DIM, POOL_GROUP_DIM)),
        _resident((1, POOL_WIDTH)),
        _resident((POOL_WIDTH, D_MODEL)),
        _resident((ATTN_WIDTH, D_MODEL)),
        _resident((D_MODEL, D_MODEL)),
        _resident((1, D_MODEL)),
    ]


def _prompt_call(x, relb, sinks, weights):
    batch, seq, _ = x.shape
    assert seq % TM == 0 and TM % WINDOW == 0 and TM >= POOL_HDR
    tile_block = pl.BlockSpec((1, TM, D_MODEL), lambda b, j: (b, j, 0))

    def seq_block(shape):
        return pl.BlockSpec((1,) + shape, lambda b, j: (b, 0, 0))

    out_shape = (
        jax.ShapeDtypeStruct((batch, seq, D_MODEL), F32),
        jax.ShapeDtypeStruct((batch, KV_WIDTH, WINDOW), F32),
        jax.ShapeDtypeStruct((batch, KV_WIDTH, WINDOW), F32),
        jax.ShapeDtypeStruct((batch, POOL_STATE_ROWS, POOL_WIDTH), F32),
    )
    scratch = [
        pltpu.VMEM((TM, D_MODEL), BF16),
        pltpu.VMEM((POOL_HDR + TM, POOL_WIDTH), F32),
        pltpu.VMEM((TM, POOL_WIDTH), F32),
        pltpu.VMEM((TM * N_Q_HEADS, LANES), BF16),
        pltpu.VMEM((WINDOW + TM, KV_WIDTH), BF16),
        pltpu.VMEM((WINDOW + TM, KV_WIDTH), BF16),
        pltpu.VMEM((TM, ATTN_WIDTH), F32),
        pltpu.VMEM((TM, D_MODEL), F32),
        pltpu.VMEM((TM, D_MODEL), F32),
        pltpu.VMEM((TM, POOL_WIDTH), BF16),
        pltpu.VMEM((TM, ATTN_WIDTH), BF16),
        pltpu.VMEM((N_Q_HEADS * WINDOW, 2 * WINDOW), F32),
    ]
    return pl.pallas_call(
        _prompt_kernel,
        grid=(batch, seq // TM),
        in_specs=[_SMEM, _SMEM, tile_block] + _weight_specs(),
        out_specs=(
            tile_block,
            seq_block((KV_WIDTH, WINDOW)),
            seq_block((KV_WIDTH, WINDOW)),
            seq_block((POOL_STATE_ROWS, POOL_WIDTH)),
        ),
        out_shape=out_shape,
        scratch_shapes=scratch,
        compiler_params=pltpu.CompilerParams(
            dimension_semantics=("arbitrary", "arbitrary"), vmem_limit_bytes=VMEM_LIMIT_BYTES),
        name="prompt_layer",
    )(relb, sinks, x, *weights)


def _sample_kernel(relb_ref, sinks_ref, x_ref, kt_ref, vt_ref, st_ref, gnorm_ref, pscale_ref, gfinal_ref,
                   w_in_ref, w_grp_ref, w_brp_ref, w_bra_ref, w_out_ref,
                   y_ref, knew_ref, vnew_ref, pnew_ref, o_in_ref, o_grp_ref, o_brp_ref, o_bra_ref, o_out_ref,
                   xn_s, h_s, u_s, zp_s, qh_s, k_s, v_s, za_s, gp_s, ga_s, pooled_s, att_s,
                   q2_s, kn2_s, vn2_s, o2_s, bias_s, wgrp_s, wbrp_s, wbra_s, wout_s):
    s = pl.program_id(0)
    n_steps = pl.num_programs(0)
    n_seq = x_ref.shape[0]
    rows_in = D_MODEL // W_STEPS

    @pl.when(s == 0)
    def _normalise():
        xn = _rmsnorm(x_ref[:, 0, :], gnorm_ref[...]).astype(BF16)
        for c in range(W_STEPS):
            xn_s[c] = xn[:, c * rows_in:(c + 1) * rows_in]
        h_s[...] = jnp.zeros((n_seq, IN_COLS), F32)

    @pl.when(s < W_STEPS)
    def _weight_chunk():
        w_in_bf = w_in_ref[0].astype(BF16)
        o_in_ref[...] = w_in_bf
        h_s[...] += _dot(xn_s[s], w_in_bf)
        for w_ref, o_ref, copy_s, axis in ((w_grp_ref, o_grp_ref, wgrp_s, 1), (w_brp_ref, o_brp_ref, wbrp_s, 0),
                                          (w_bra_ref, o_bra_ref, wbra_s, 0), (w_out_ref, o_out_ref, wout_s, 0)):
            w_bf = w_ref[0].astype(BF16)
            o_ref[...] = w_bf
            rows = w_bf.shape[axis]
            chunk_rows = pl.ds(pl.multiple_of(s * rows, rows), rows)
            if axis == 0:
                copy_s[chunk_rows, :] = w_bf
            else:
                copy_s[:, chunk_rows, :] = w_bf

    @pl.when(s == W_STEPS - 1)
    def _activations():
        slot = lax.broadcasted_iota(jnp.int32, (SUBLANES, WINDOW), 1)
        head = lax.broadcasted_iota(jnp.int32, (SUBLANES, WINDOW), 0)
        tbl = jnp.zeros((SUBLANES, WINDOW), F32)
        for h in range(N_Q_HEADS):
            tbl = jnp.where(head == h, _bias_table(WINDOW - slot, relb_ref, h), tbl)
        bias_s[...] = tbl

        u_s[...] = h_s[:, OFF_U:OFF_U + POOL_WIDTH]
        zp_s[...] = _silu(h_s[:, OFF_ZP:OFF_ZP + POOL_WIDTH])
        q_heads = _head_blocks(h_s[:, OFF_Q:OFF_Q + ATTN_WIDTH] * (HEAD_DIM ** -0.5))
        for h in range(N_Q_HEADS):
            qh_s[:, h * LANES:(h + 1) * LANES] = q_heads[h]
        k_s[...] = h_s[:, OFF_K:OFF_K + KV_WIDTH]
        v_s[...] = h_s[:, OFF_V:OFF_V + KV_WIDTH]
        za_s[...] = _silu(h_s[:, OFF_ZA:OFF_ZA + ATTN_WIDTH])
        gp_s[...] = _sigmoid(h_s[:, OFF_GP:OFF_GP + D_MODEL])
        ga_s[...] = _sigmoid(h_s[:, OFF_GA:OFF_GA + D_MODEL])

    @pl.when(s >= W_STEPS)
    def _sequence_chunk():
        i = s - W_STEPS
        chunk = pl.ds(pl.multiple_of(i * BC, BC), BC)

        u_new = u_s[chunk, :]
        n_hist = POOL_STATE_ROWS
        parts = []
        for g, w in enumerate(POOL_WINDOWS):
            cols = slice(g * POOL_GROUP_DIM, (g + 1) * POOL_GROUP_DIM)
            acc = u_new[:, cols]
            for r in range(n_hist - (w - 1), n_hist):
                acc = acc + st_ref[r, :, cols]
            parts.append(acc * (1.0 / w) - u_new[:, cols])
        pooled_s[chunk, :] = jnp.concatenate(parts, axis=-1).astype(BF16)
        pnew_ref[0:n_hist - 1] = st_ref[1:n_hist]
        pnew_ref[n_hist - 1] = u_new

        k_new = k_s[chunk, :]
        v_new = v_s[chunk, :]
        last_slot = lax.broadcasted_iota(jnp.int32, (KV_WIDTH, WINDOW), 1) == WINDOW - 1
        k_new_t = k_new.T
        v_new_t = v_new.T
        for b in range(BC):
            knew_ref[b] = jnp.where(last_slot, k_new_t[:, b:b + 1], pltpu.roll(kt_ref[b], WINDOW - 1, 1))
            vnew_ref[b] = jnp.where(last_slot, v_new_t[:, b:b + 1], pltpu.roll(vt_ref[b], WINDOW - 1, 1))

        for h in range(N_Q_HEADS):
            q2_s[pl.ds(h, BC, stride=N_Q_HEADS), :] = qh_s[chunk, h * LANES:(h + 1) * LANES]
            kn2_s[pl.ds(h, BC, stride=N_Q_HEADS), :] = k_new
            vn2_s[pl.ds(h, BC, stride=N_Q_HEADS), :] = v_new
        q3 = q2_s[...].reshape(BC, N_Q_HEADS, LANES)
        kn3 = kn2_s[...].reshape(BC, N_Q_HEADS, LANES)
        vn3 = vn2_s[...].reshape(BC, N_Q_HEADS, LANES)
        s3 = lax.dot_general(q3.astype(BF16), kt_ref[...].astype(BF16), (((2,), (1,)), ((0,), (0,))),
                             preferred_element_type=F32) + bias_s[...][None]
        s_self = jnp.sum(q3 * kn3, axis=-1, keepdims=True) + _per_head_column(lambda h: relb_ref[h, 0])[None]
        sink = _per_head_column(lambda h: sinks_ref[h])[None]
        m = jnp.maximum(jnp.maximum(jnp.max(s3, axis=-1, keepdims=True), s_self), sink)
        p3 = jnp.exp(s3 - m)
        p_self = jnp.exp(s_self - m)
        denom = jnp.sum(p3, axis=-1, keepdims=True) + p_self + jnp.exp(sink - m)
        o3 = lax.dot_general(p3.astype(BF16), vt_ref[...].astype(BF16), (((2,), (2,)), ((0,), (0,))),
                             preferred_element_type=F32)
        o3 = (o3 + p_self * vn3) * (1.0 / denom)
        o2_s[...] = o3.reshape(BC * N_Q_HEADS, LANES)
        outs = [o2_s[pl.ds(h, BC, stride=N_Q_HEADS), :] for h in range(N_Q_HEADS)]
        att_s[chunk, :] = (_gather_heads(outs) * za_s[chunk, :]).astype(BF16)

    @pl.when(s == n_steps - 1)
    def _merge_all():
        out = _residual_sum(x_ref[:, 0, :], pooled_s[...], zp_s[...], att_s[...], gp_s[...], ga_s[...],
                            wgrp_s, pscale_ref[...], wbrp_s, wbra_s, wout_s)
        y_ref[:, 0, :] = _rmsnorm(out, gfinal_ref[...])


def _sample_call(x, cache_kt, cache_vt, state_t, relb, sinks, g_norm, pool_scale, g_final,
                 w_in, w_grp, w_brp, w_bra, w_out):
    n_seq = x.shape[0]
    assert n_seq % BC == 0

    def chunk_rows(n):
        assert n % (W_STEPS * 2 * SUBLANES) == 0
        return n // W_STEPS

    r_in, r_grp, r_br, r_out = (chunk_rows(D_MODEL), chunk_rows(POOL_GROUP_DIM), chunk_rows(POOL_WIDTH),
                                chunk_rows(D_MODEL))

    def w_step(s):
        return jnp.minimum(s, W_STEPS - 1)

    def seq_step(s):
        return jnp.maximum(s - W_STEPS, 0)

    weight_in_specs = [
        pl.BlockSpec((1, r_in, IN_COLS), lambda s: (0, w_step(s), 0)),
        pl.BlockSpec((1, POOL_GROUPS, r_grp, POOL_GROUP_DIM), lambda s: (0, 0, w_step(s), 0)),
        pl.BlockSpec((1, r_br, D_MODEL), lambda s: (0, w_step(s), 0)),
        pl.BlockSpec((1, r_br, D_MODEL), lambda s: (0, w_step(s), 0)),
        pl.BlockSpec((1, r_out, D_MODEL), lambda s: (0, w_step(s), 0)),
    ]
    weight_out_specs = (
        pl.BlockSpec((r_in, IN_COLS), lambda s: (w_step(s), 0)),
        pl.BlockSpec((POOL_GROUPS, r_grp, POOL_GROUP_DIM), lambda s: (0, w_step(s), 0)),
        pl.BlockSpec((r_br, D_MODEL), lambda s: (w_step(s), 0)),
        pl.BlockSpec((r_br, D_MODEL), lambda s: (w_step(s), 0)),
        pl.BlockSpec((r_out, D_MODEL), lambda s: (w_step(s), 0)),
    )
    out_shape = (
        jax.ShapeDtypeStruct((n_seq, 1, D_MODEL), F32),
        jax.ShapeDtypeStruct((n_seq, KV_WIDTH, WINDOW), F32),
        jax.ShapeDtypeStruct((n_seq, KV_WIDTH, WINDOW), F32),
        jax.ShapeDtypeStruct((POOL_STATE_ROWS, n_seq, POOL_WIDTH), F32),
        jax.ShapeDtypeStruct((D_MODEL, IN_COLS), BF16),
        jax.ShapeDtypeStruct((POOL_GROUPS, POOL_GROUP_DIM, POOL_GROUP_DIM), BF16),
        jax.ShapeDtypeStruct((POOL_WIDTH, D_MODEL), BF16),
        jax.ShapeDtypeStruct((ATTN_WIDTH, D_MODEL), BF16),
        jax.ShapeDtypeStruct((D_MODEL, D_MODEL), BF16),
    )
    scratch = [
        pltpu.VMEM((W_STEPS, n_seq, D_MODEL // W_STEPS), BF16),
        pltpu.VMEM((n_seq, IN_COLS), F32),
        pltpu.VMEM((n_seq, POOL_WIDTH), F32),
        pltpu.VMEM((n_seq, POOL_WIDTH), F32),
        pltpu.VMEM((n_seq, N_Q_HEADS * LANES), F32),
        pltpu.VMEM((n_seq, KV_WIDTH), F32),
        pltpu.VMEM((n_seq, KV_WIDTH), F32),
        pltpu.VMEM((n_seq, ATTN_WIDTH), F32),
        pltpu.VMEM((n_seq, D_MODEL), F32),
        pltpu.VMEM((n_seq, D_MODEL), F32),
        pltpu.VMEM((n_seq, POOL_WIDTH), BF16),
        pltpu.VMEM((n_seq, ATTN_WIDTH), BF16),
        pltpu.VMEM((BC * N_Q_HEADS, LANES), F32),
        pltpu.VMEM((BC * N_Q_HEADS, LANES), F32),
        pltpu.VMEM((BC * N_Q_HEADS, LANES), F32),
        pltpu.VMEM((BC * N_Q_HEADS, LANES), F32),
        pltpu.VMEM((SUBLANES, WINDOW), F32),
        pltpu.VMEM((POOL_GROUPS, POOL_GROUP_DIM, POOL_GROUP_DIM), BF16),
        pltpu.VMEM((POOL_WIDTH, D_MODEL), BF16),
        pltpu.VMEM((ATTN_WIDTH, D_MODEL), BF16),
        pltpu.VMEM((D_MODEL, D_MODEL), BF16),
    ]
    cache = pl.BlockSpec((BC, KV_WIDTH, WINDOW), lambda s: (seq_step(s), 0, 0))
    hist = pl.BlockSpec((POOL_STATE_ROWS, BC, POOL_WIDTH), lambda s: (0, seq_step(s), 0))
    return pl.pallas_call(
        _sample_kernel,
        grid=(W_STEPS + n_seq // BC,),
        in_specs=[_SMEM, _SMEM, _resident((n_seq, 1, D_MODEL)), cache, cache, hist,
                  _resident((1, D_MODEL)), _resident((1, POOL_WIDTH)), _resident((1, D_MODEL))] + weight_in_specs,
        out_specs=(pl.BlockSpec((n_seq, 1, D_MODEL), lambda s: (0, 0, 0)), cache, cache, hist) + weight_out_specs,
        out_shape=out_shape,
        scratch_shapes=scratch,
        compiler_params=pltpu.CompilerParams(
            dimension_semantics=("arbitrary",), vmem_limit_bytes=VMEM_LIMIT_BYTES),
        name="sample_layer",
    )(relb, sinks, x, cache_kt, cache_vt, state_t, g_norm, pool_scale, g_final, w_in, w_grp, w_brp, w_bra, w_out)


def _cache_as_kd_slot(cache):
    n = cache.shape[0]
    return cache.transpose(0, 2, 3, 1).reshape(n, KV_WIDTH, WINDOW)


def _cache_from_kd_slot(cache_t):
    n = cache_t.shape[0]
    return cache_t.reshape(n, N_KV_HEADS, HEAD_DIM, WINDOW).transpose(0, 3, 1, 2)


def kernel(x_prompt, x_sample, cache_k, cache_v, state_pool, rel_bias, g_norm, w_in, pool_w_grp, pool_scale, attn_sinks, w_br_pool, w_br_attn, w_out, g_final):
    depth = g_norm.shape[0]
    assert depth == 1 and x_sample.shape[1] == 1
    l = 0
    sinks = attn_sinks[l]
    relb = rel_bias.T
    g_in, p_scale, g_out = g_norm[l].reshape(1, D_MODEL), pool_scale[l].reshape(1, POOL_WIDTH), g_final.reshape(1, D_MODEL)

    y_s, k_s, v_s, pool_s, w_in_bf, w_grp_bf, w_brp_bf, w_bra_bf, w_out_bf = _sample_call(
        x_sample, _cache_as_kd_slot(cache_k[l]), _cache_as_kd_slot(cache_v[l]),
        state_pool[l].transpose(1, 0, 2), relb, sinks, g_in, p_scale, g_out,
        w_in, pool_w_grp, w_br_pool, w_br_attn, w_out)
    weights = (g_in, w_in_bf, w_grp_bf, p_scale, w_brp_bf, w_bra_bf, w_out_bf, g_out)
    y_p, k_p, v_p, pool_p = _prompt_call(x_prompt, relb, sinks, weights)
    return (
        y_p,
        y_s,
        _cache_from_kd_slot(k_p)[None],
        _cache_from_kd_slot(v_p)[None],
        pool_p[None],
        _cache_from_kd_slot(k_s)[None],
        _cache_from_kd_slot(v_s)[None],
        pool_s.transpose(1, 0, 2)[None],
    )
```

```python
import math

import jax
import jax.numpy as jnp
from jax import lax
from jax.experimental import pallas as pl
from jax.experimental.pallas import tpu as pltpu

D_MODEL = 1024
POOL_WINDOWS = (2, 4, 8, 16)
POOL_GROUPS = len(POOL_WINDOWS)
POOL_WIDTH = D_MODEL // 2
POOL_GROUP_DIM = POOL_WIDTH // POOL_GROUPS
POOL_STATE_ROWS = max(POOL_WINDOWS) - 1
HEAD_DIM = 64
N_KV_HEADS = 2
ATTN_WIDTH = D_MODEL // 2
N_Q_HEADS = ATTN_WIDTH // HEAD_DIM
GQA_GROUP = N_Q_HEADS // N_KV_HEADS
KV_WIDTH = N_KV_HEADS * HEAD_DIM
WINDOW = 128
N_BUCKETS = 32
MAX_DISTANCE = 128
RMS_EPS = 1e-6
IN_COLS = 2 * POOL_WIDTH + 2 * ATTN_WIDTH + 2 * KV_WIDTH + 2 * D_MODEL

OFF_U = 0
OFF_ZP = OFF_U + POOL_WIDTH
OFF_Q = OFF_ZP + POOL_WIDTH
OFF_K = OFF_Q + ATTN_WIDTH
OFF_V = OFF_K + KV_WIDTH
OFF_ZA = OFF_V + KV_WIDTH
OFF_GP = OFF_ZA + ATTN_WIDTH
OFF_GA = OFF_GP + D_MODEL

LANES = 128
SUBLANES = 8
POOL_HDR = 16
TM = 1024
BC = 32
W_STEPS = 4
VMEM_LIMIT_BYTES = 56 * 1024 * 1024

assert KV_WIDTH == LANES and 2 * HEAD_DIM == LANES and GQA_GROUP % 2 == 0

F32 = jnp.float32
BF16 = jnp.bfloat16
NEG_INF = float("-inf")


def _sigmoid(x):
    return 1.0 / (1.0 + jnp.exp(-x))


def _silu(x):
    return x * _sigmoid(x)


def _rmsnorm(x, g):
    return x * lax.rsqrt(jnp.mean(x * x, axis=-1, keepdims=True) + RMS_EPS) * g


def _dot(a, b):
    return jnp.dot(a, b, preferred_element_type=F32)


def _t5_bucket(rel):
    n = jnp.maximum(rel, 0)
    max_exact = N_BUCKETS // 2
    nf = jnp.maximum(n, 1).astype(F32)
    v = jnp.log(nf / max_exact) / math.log(MAX_DISTANCE / max_exact) * (N_BUCKETS - max_exact)
    steps = jnp.zeros(rel.shape, jnp.int32)
    for k in range(1, N_BUCKETS - max_exact):
        steps = steps + jnp.where(v >= k, 1, 0)
    return jnp.where(n < max_exact, n, max_exact + steps)


def _bias_table(rel, relb_ref, h):
    bucket = _t5_bucket(rel)

    def body(b, tbl):
        return jnp.where(bucket == b, relb_ref[h, b], tbl)

    tbl = lax.fori_loop(0, N_BUCKETS, body, jnp.zeros(rel.shape, F32))
    return jnp.where((rel >= 0) & (rel < WINDOW), tbl, NEG_INF)


def _per_head_column(ref_scalar):
    head = lax.broadcasted_iota(jnp.int32, (N_Q_HEADS, 1), 0)
    col = jnp.zeros((N_Q_HEADS, 1), F32)
    for h in range(N_Q_HEADS):
        col = jnp.where(head == h, ref_scalar(h), col)
    return col


def _project(xn_ref, w_in_ref, off, width):
    return _dot(xn_ref[...], w_in_ref[:, off:off + width])


def _kv_lane_mask(shape):
    return lax.broadcasted_iota(jnp.int32, shape, len(shape) - 1) < HEAD_DIM


def _head_pairs():
    for grp in range(ATTN_WIDTH // LANES):
        kv, pair = divmod(grp, GQA_GROUP // 2)
        yield grp, kv, kv * GQA_GROUP + 2 * pair, kv * GQA_GROUP + 2 * pair + 1


def _head_blocks(q):
    low = _kv_lane_mask((q.shape[0], LANES))
    blocks = [None] * N_Q_HEADS
    for grp, kv, even, odd in _head_pairs():
        a = q[:, grp * LANES:(grp + 1) * LANES]
        swapped = pltpu.roll(a, HEAD_DIM, 1)
        if kv == 0:
            blocks[even], blocks[odd] = jnp.where(low, a, 0.0), jnp.where(low, swapped, 0.0)
        else:
            blocks[even], blocks[odd] = jnp.where(low, 0.0, swapped), jnp.where(low, 0.0, a)
    return blocks


def _gather_heads(outs):
    low = _kv_lane_mask(outs[0].shape)
    groups = []
    for _, kv, even, odd in _head_pairs():
        if kv == 0:
            groups.append(jnp.where(low, outs[even], pltpu.roll(outs[odd], HEAD_DIM, 1)))
        else:
            groups.append(jnp.where(low, pltpu.roll(outs[even], HEAD_DIM, 1), outs[odd]))
    return jnp.concatenate(groups, axis=-1)


def _pool_windows(x_ext, inv_cnt):
    outs = []
    for g, w in enumerate(POOL_WINDOWS):
        xg = x_ext[:, g * POOL_GROUP_DIM:(g + 1) * POOL_GROUP_DIM]
        acc = xg
        shift = 1
        while shift < w:
            acc = acc + pltpu.roll(acc, shift, 0)
            shift *= 2
        outs.append(acc[POOL_HDR:] * inv_cnt[g] - xg[POOL_HDR:])
    return jnp.concatenate(outs, axis=-1)


def _residual_sum(x, pooled, zp_act, att_act, sig_gp, sig_ga, w_grp_ref, pool_scale, w_brp_ref, w_bra_ref,
                  w_out_ref):
    pg = jnp.concatenate(
        [_dot(pooled[:, g * POOL_GROUP_DIM:(g + 1) * POOL_GROUP_DIM], w_grp_ref[g])
         for g in range(POOL_GROUPS)], axis=-1)
    br_pool = _dot((pg * pool_scale * zp_act).astype(BF16), w_brp_ref[...])
    br_attn = _dot(att_act, w_bra_ref[...])
    merged = sig_gp * br_pool + sig_ga * br_attn
    return x + _dot(merged.astype(BF16), w_out_ref[...])


def _prompt_kernel(relb_ref, sinks_ref, x_ref, gnorm_ref, w_in_ref, w_grp_ref, pscale_ref,
                   w_brp_ref, w_bra_ref, w_out_ref, gfinal_ref,
                   y_ref, knew_ref, vnew_ref, pnew_ref,
                   xn_s, uext_s, zp_s, qh_s, kband_s, vband_s, za_s, gp_s, ga_s, pooled_s, att_s, bias_s):
    b = pl.program_id(0)
    j = pl.program_id(1)
    nblk = TM // WINDOW
    hrows = N_Q_HEADS * WINDOW

    @pl.when((b == 0) & (j == 0))
    def _build_bias():
        qi = lax.broadcasted_iota(jnp.int32, (WINDOW, 2 * WINDOW), 0)
        kc = lax.broadcasted_iota(jnp.int32, (WINDOW, 2 * WINDOW), 1)
        for h in range(N_Q_HEADS):
            bias_s[h * WINDOW:(h + 1) * WINDOW, :] = _bias_table(qi - kc + WINDOW, relb_ref, h)

    @pl.when(j == 0)
    def _reset_carry():
        uext_s[0:POOL_HDR, :] = jnp.zeros((POOL_HDR, POOL_WIDTH), F32)
        kband_s[0:WINDOW, :] = jnp.zeros((WINDOW, KV_WIDTH), BF16)
        vband_s[0:WINDOW, :] = jnp.zeros((WINDOW, KV_WIDTH), BF16)

    def _tile():
        xn_s[...] = _rmsnorm(x_ref[0], gnorm_ref[...]).astype(BF16)
        uext_s[POOL_HDR:, :] = _project(xn_s, w_in_ref, OFF_U, POOL_WIDTH)
        q_heads = _head_blocks(_project(xn_s, w_in_ref, OFF_Q, ATTN_WIDTH) * (HEAD_DIM ** -0.5))
        for c in range(nblk):
            for h in range(N_Q_HEADS):
                r0 = c * hrows + h * WINDOW
                qh_s[r0:r0 + WINDOW, :] = q_heads[h][c * WINDOW:(c + 1) * WINDOW].astype(BF16)

        pos = j * TM + lax.broadcasted_iota(jnp.int32, (TM, 1), 0)
        inv_cnt = [1.0 / jnp.minimum(pos + 1, w).astype(F32) for w in POOL_WINDOWS]
        x_ext = uext_s[...]
        pooled_s[...] = _pool_windows(x_ext, inv_cnt).astype(BF16)
        pnew_ref[0] = x_ext[POOL_HDR + TM - POOL_STATE_ROWS:, :]
        uext_s[0:POOL_HDR, :] = x_ext[TM:, :]

        kv = _project(xn_s, w_in_ref, OFF_K, 2 * KV_WIDTH)
        kband_s[WINDOW:, :] = kv[:, :KV_WIDTH].astype(BF16)
        vband_s[WINDOW:, :] = kv[:, KV_WIDTH:].astype(BF16)
        knew_ref[0] = kv[TM - WINDOW:, :KV_WIDTH].T
        vnew_ref[0] = kv[TM - WINDOW:, KV_WIDTH:].T
        za_s[...] = _silu(_project(xn_s, w_in_ref, OFF_ZA, ATTN_WIDTH))

        first_neg = jnp.where(j == 0, NEG_INF, 0.0).astype(F32)
        for c in range(nblk):
            rows = slice(c * WINDOW, (c + 1) * WINDOW)
            k_c = kband_s[c * WINDOW:(c + 2) * WINDOW, :]
            v_c = vband_s[c * WINDOW:(c + 2) * WINDOW, :]
            s_all = lax.dot_general(qh_s[c * hrows:(c + 1) * hrows, :], k_c, (((1,), (1,)), ((), ())),
                                    preferred_element_type=F32)
            p_parts, inv_parts = [], []
            for h in range(N_Q_HEADS):
                head_rows = slice(h * WINDOW, (h + 1) * WINDOW)
                sc = s_all[head_rows, :] + bias_s[head_rows, :]
                if c == 0:
                    sc = jnp.concatenate([sc[:, :WINDOW] + first_neg, sc[:, WINDOW:]], axis=-1)
                sink = sinks_ref[h]
                m = jnp.maximum(jnp.max(sc, axis=-1, keepdims=True), sink)
                p = jnp.exp(sc - m)
                denom = jnp.sum(p, axis=-1, keepdims=True) + jnp.exp(sink - m)
                p_parts.append(p.astype(BF16))
                inv_parts.append(1.0 / denom)
            o_all = _dot(jnp.concatenate(p_parts, axis=0), v_c)
            outs = [o_all[h * WINDOW:(h + 1) * WINDOW, :] * inv_parts[h] for h in range(N_Q_HEADS)]
            att_s[rows, :] = (_gather_heads(outs) * za_s[rows, :]).astype(BF16)
        kband_s[0:WINDOW, :] = kband_s[TM:, :]
        vband_s[0:WINDOW, :] = vband_s[TM:, :]

        zp_s[...] = _silu(_project(xn_s, w_in_ref, OFF_ZP, POOL_WIDTH))
        for half in range(2):
            cols = slice(half * POOL_WIDTH, (half + 1) * POOL_WIDTH)
            gp_s[:, cols] = _sigmoid(_project(xn_s, w_in_ref, OFF_GP + half * POOL_WIDTH, POOL_WIDTH))
            ga_s[:, cols] = _sigmoid(_project(xn_s, w_in_ref, OFF_GA + half * POOL_WIDTH, POOL_WIDTH))

        out = _residual_sum(x_ref[0], pooled_s[...], zp_s[...], att_s[...], gp_s[...], ga_s[...],
                            w_grp_ref, pscale_ref[...], w_brp_ref, w_bra_ref, w_out_ref)
        y_ref[0] = _rmsnorm(out, gfinal_ref[...])

    _tile()


def _resident(shape):
    return pl.BlockSpec(shape, lambda *_: (0,) * len(shape), pipeline_mode=pl.Buffered(1))


_SMEM = pl.BlockSpec(memory_space=pltpu.SMEM)


def _weight_specs():
    return [
        _resident((1, D_MODEL)),
        _resident((D_MODEL, IN_COLS)),
        _resident((POOL_GROUPS, POOL_GROUP_DIM, POOL_GROUP_DIM)),
        _resident((1, POOL_WIDTH)),
        _resident((POOL_WIDTH, D_MODEL)),
        _resident((ATTN_WIDTH, D_MODEL)),
        _resident((D_MODEL, D_MODEL)),
        _resident((1, D_MODEL)),
    ]


def _prompt_call(x, relb, sinks, weights):
    batch, seq, _ = x.shape
    assert seq % TM == 0 and TM % WINDOW == 0 and TM >= POOL_HDR
    tile_block = pl.BlockSpec((1, TM, D_MODEL), lambda b, j: (b, j, 0))

    def seq_block(shape):
        return pl.BlockSpec((1,) + shape, lambda b, j: (b, 0, 0))

    out_shape = (
        jax.ShapeDtypeStruct((batch, seq, D_MODEL), F32),
        jax.ShapeDtypeStruct((batch, KV_WIDTH, WINDOW), F32),
        jax.ShapeDtypeStruct((batch, KV_WIDTH, WINDOW), F32),
        jax.ShapeDtypeStruct((batch, POOL_STATE_ROWS, POOL_WIDTH), F32),
    )
    scratch = [
        pltpu.VMEM((TM, D_MODEL), BF16),
        pltpu.VMEM((POOL_HDR + TM, POOL_WIDTH), F32),
        pltpu.VMEM((TM, POOL_WIDTH), F32),
        pltpu.VMEM((TM * N_Q_HEADS, LANES), BF16),
        pltpu.VMEM((WINDOW + TM, KV_WIDTH), BF16),
        pltpu.VMEM((WINDOW + TM, KV_WIDTH), BF16),
        pltpu.VMEM((TM, ATTN_WIDTH), F32),
        pltpu.VMEM((TM, D_MODEL), F32),
        pltpu.VMEM((TM, D_MODEL), F32),
        pltpu.VMEM((TM, POOL_WIDTH), BF16),
        pltpu.VMEM((TM, ATTN_WIDTH), BF16),
        pltpu.VMEM((N_Q_HEADS * WINDOW, 2 * WINDOW), F32),
    ]
    return pl.pallas_call(
        _prompt_kernel,
        grid=(batch, seq // TM),
        in_specs=[_SMEM, _SMEM, tile_block] + _weight_specs(),
        out_specs=(
            tile_block,
            seq_block((KV_WIDTH, WINDOW)),
            seq_block((KV_WIDTH, WINDOW)),
            seq_block((POOL_STATE_ROWS, POOL_WIDTH)),
        ),
        out_shape=out_shape,
        scratch_shapes=scratch,
        compiler_params=pltpu.CompilerParams(
            dimension_semantics=("arbitrary", "arbitrary"), vmem_limit_bytes=VMEM_LIMIT_BYTES),
        name="prompt_layer",
    )(relb, sinks, x, *weights)


def _sample_kernel(relb_ref, sinks_ref, x_ref, kt_ref, vt_ref, st_ref, gnorm_ref, pscale_ref, gfinal_ref,
                   w_in_ref, w_grp_ref, w_brp_ref, w_bra_ref, w_out_ref,
                   y_ref, knew_ref, vnew_ref, pnew_ref, o_in_ref, o_grp_ref, o_brp_ref, o_bra_ref, o_out_ref,
                   xn_s, h_s, u_s, zp_s, qh_s, k_s, v_s, za_s, gp_s, ga_s, pooled_s, att_s,
                   q2_s, kn2_s, vn2_s, o2_s, bias_s, wgrp_s, wbrp_s, wbra_s, wout_s):
    s = pl.program_id(0)
    n_steps = pl.num_programs(0)
    n_seq = x_ref.shape[0]
    rows_in = D_MODEL // W_STEPS

    @pl.when(s == 0)
    def _normalise():
        xn = _rmsnorm(x_ref[:, 0, :], gnorm_ref[...]).astype(BF16)
        for c in range(W_STEPS):
            xn_s[c] = xn[:, c * rows_in:(c + 1) * rows_in]
        h_s[...] = jnp.zeros((n_seq, IN_COLS), F32)

    @pl.when(s < W_STEPS)
    def _weight_chunk():
        w_in_bf = w_in_ref[0].astype(BF16)
        o_in_ref[...] = w_in_bf
        h_s[...] += _dot(xn_s[s], w_in_bf)
        for w_ref, o_ref, copy_s, axis in ((w_grp_ref, o_grp_ref, wgrp_s, 1), (w_brp_ref, o_brp_ref, wbrp_s, 0),
                                          (w_bra_ref, o_bra_ref, wbra_s, 0), (w_out_ref, o_out_ref, wout_s, 0)):
            w_bf = w_ref[0].astype(BF16)
            o_ref[...] = w_bf
            rows = w_bf.shape[axis]
            chunk_rows = pl.ds(pl.multiple_of(s * rows, rows), rows)
            if axis == 0:
                copy_s[chunk_rows, :] = w_bf
            else:
                copy_s[:, chunk_rows, :] = w_bf

    @pl.when(s == W_STEPS - 1)
    def _activations():
        slot = lax.broadcasted_iota(jnp.int32, (SUBLANES, WINDOW), 1)
        head = lax.broadcasted_iota(jnp.int32, (SUBLANES, WINDOW), 0)
        tbl = jnp.zeros((SUBLANES, WINDOW), F32)
        for h in range(N_Q_HEADS):
            tbl = jnp.where(head == h, _bias_table(WINDOW - slot, relb_ref, h), tbl)
        bias_s[...] = tbl

        u_s[...] = h_s[:, OFF_U:OFF_U + POOL_WIDTH]
        zp_s[...] = _silu(h_s[:, OFF_ZP:OFF_ZP + POOL_WIDTH])
        q_heads = _head_blocks(h_s[:, OFF_Q:OFF_Q + ATTN_WIDTH] * (HEAD_DIM ** -0.5))
        for h in range(N_Q_HEADS):
            qh_s[:, h * LANES:(h + 1) * LANES] = q_heads[h]
        k_s[...] = h_s[:, OFF_K:OFF_K + KV_WIDTH]
        v_s[...] = h_s[:, OFF_V:OFF_V + KV_WIDTH]
        za_s[...] = _silu(h_s[:, OFF_ZA:OFF_ZA + ATTN_WIDTH])
        gp_s[...] = _sigmoid(h_s[:, OFF_GP:OFF_GP + D_MODEL])
        ga_s[...] = _sigmoid(h_s[:, OFF_GA:OFF_GA + D_MODEL])

    @pl.when(s >= W_STEPS)
    def _sequence_chunk():
        i = s - W_STEPS
        chunk = pl.ds(pl.multiple_of(i * BC, BC), BC)

        u_new = u_s[chunk, :]
        n_hist = POOL_STATE_ROWS
        parts = []
        for g, w in enumerate(POOL_WINDOWS):
            cols = slice(g * POOL_GROUP_DIM, (g + 1) * POOL_GROUP_DIM)
            acc = u_new[:, cols]
            for r in range(n_hist - (w - 1), n_hist):
                acc = acc + st_ref[r, :, cols]
            parts.append(acc * (1.0 / w) - u_new[:, cols])
        pooled_s[chunk, :] = jnp.concatenate(parts, axis=-1).astype(BF16)
        pnew_ref[0:n_hist - 1] = st_ref[1:n_hist]
        pnew_ref[n_hist - 1] = u_new

        k_new = k_s[chunk, :]
        v_new = v_s[chunk, :]
        last_slot = lax.broadcasted_iota(jnp.int32, (KV_WIDTH, WINDOW), 1) == WINDOW - 1
        k_new_t = k_new.T
        v_new_t = v_new.T
        for b in range(BC):
            knew_ref[b] = jnp.where(last_slot, k_new_t[:, b:b + 1], pltpu.roll(kt_ref[b], WINDOW - 1, 1))
            vnew_ref[b] = jnp.where(last_slot, v_new_t[:, b:b + 1], pltpu.roll(vt_ref[b], WINDOW - 1, 1))

        for h in range(N_Q_HEADS):
            q2_s[pl.ds(h, BC, stride=N_Q_HEADS), :] = qh_s[chunk, h * LANES:(h + 1) * LANES]
            kn2_s[pl.ds(h, BC, stride=N_Q_HEADS), :] = k_new
            vn2_s[pl.ds(h, BC, stride=N_Q_HEADS), :] = v_new
        q3 = q2_s[...].reshape(BC, N_Q_HEADS, LANES)
        kn3 = kn2_s[...].reshape(BC, N_Q_HEADS, LANES)
        vn3 = vn2_s[...].reshape(BC, N_Q_HEADS, LANES)
        s3 = lax.dot_general(q3.astype(BF16), kt_ref[...].astype(BF16), (((2,), (1,)), ((0,), (0,))),
                             preferred_element_type=F32) + bias_s[...][None]
        s_self = jnp.sum(q3 * kn3, axis=-1, keepdims=True) + _per_head_column(lambda h: relb_ref[h, 0])[None]
        sink = _per_head_column(lambda h: sinks_ref[h])[None]
        m = jnp.maximum(jnp.maximum(jnp.max(s3, axis=-1, keepdims=True), s_self), sink)
        p3 = jnp.exp(s3 - m)
        p_self = jnp.exp(s_self - m)
        denom = jnp.sum(p3, axis=-1, keepdims=True) + p_self + jnp.exp(sink - m)
        o3 = lax.dot_general(p3.astype(BF16), vt_ref[...].astype(BF16), (((2,), (2,)), ((0,), (0,))),
                             preferred_element_type=F32)
        o3 = (o3 + p_self * vn3) * (1.0 / denom)
        o2_s[...] = o3.reshape(BC * N_Q_HEADS, LANES)
        outs = [o2_s[pl.ds(h, BC, stride=N_Q_HEADS), :] for h in range(N_Q_HEADS)]
        att_s[chunk, :] = (_gather_heads(outs) * za_s[chunk, :]).astype(BF16)

    @pl.when(s == n_steps - 1)
    def _merge_all():
        out = _residual_sum(x_ref[:, 0, :], pooled_s[...], zp_s[...], att_s[...], gp_s[...], ga_s[...],
                            wgrp_s, pscale_ref[...], wbrp_s, wbra_s, wout_s)
        y_ref[:, 0, :] = _rmsnorm(out, gfinal_ref[...])


def _sample_call(x, cache_kt, cache_vt, state_t, relb, sinks, g_norm, pool_scale, g_final,
                 w_in, w_grp, w_brp, w_bra, w_out):
    n_seq = x.shape[0]
    assert n_seq % BC == 0

    def chunk_rows(n):
        assert n % (W_STEPS * 2 * SUBLANES) == 0
        return n // W_STEPS

    r_in, r_grp, r_br, r_out = (chunk_rows(D_MODEL), chunk_rows(POOL_GROUP_DIM), chunk_rows(POOL_WIDTH),
                                chunk_rows(D_MODEL))

    def w_step(s):
        return jnp.minimum(s, W_STEPS - 1)

    def seq_step(s):
        return jnp.maximum(s - W_STEPS, 0)

    weight_in_specs = [
        pl.BlockSpec((1, r_in, IN_COLS), lambda s: (0, w_step(s), 0)),
        pl.BlockSpec((1, POOL_GROUPS, r_grp, POOL_GROUP_DIM), lambda s: (0, 0, w_step(s), 0)),
        pl.BlockSpec((1, r_br, D_MODEL), lambda s: (0, w_step(s), 0)),
        pl.BlockSpec((1, r_br, D_MODEL), lambda s: (0, w_step(s), 0)),
        pl.BlockSpec((1, r_out, D_MODEL), lambda s: (0, w_step(s), 0)),
    ]
    weight_out_specs = (
        pl.BlockSpec((r_in, IN_COLS), lambda s: (w_step(s), 0)),
        pl.BlockSpec((POOL_GROUPS, r_grp, POOL_GROUP_DIM), lambda s: (0, w_step(s), 0)),
        pl.BlockSpec((r_br, D_MODEL), lambda s: (w_step(s), 0)),
        pl.BlockSpec((r_br, D_MODEL), lambda s: (w_step(s), 0)),
        pl.BlockSpec((r_out, D_MODEL), lambda s: (w_step(s), 0)),
    )
    out_shape = (
        jax.ShapeDtypeStruct((n_seq, 1, D_MODEL), F32),
        jax.ShapeDtypeStruct((n_seq, KV_WIDTH, WINDOW), F32),
        jax.ShapeDtypeStruct((n_seq, KV_WIDTH, WINDOW), F32),
        jax.ShapeDtypeStruct((POOL_STATE_ROWS, n_seq, POOL_WIDTH), F32),
        jax.ShapeDtypeStruct((D_MODEL, IN_COLS), BF16),
        jax.ShapeDtypeStruct((POOL_GROUPS, POOL_GROUP_DIM, POOL_GROUP_DIM), BF16),
        jax.ShapeDtypeStruct((POOL_WIDTH, D_MODEL), BF16),
        jax.ShapeDtypeStruct((ATTN_WIDTH, D_MODEL), BF16),
        jax.ShapeDtypeStruct((D_MODEL, D_MODEL), BF16),
    )
    scratch = [
        pltpu.VMEM((W_STEPS, n_seq, D_MODEL // W_STEPS), BF16),
        pltpu.VMEM((n_seq, IN_COLS), F32),
        pltpu.VMEM((n_seq, POOL_WIDTH), F32),
        pltpu.VMEM((n_seq, POOL_WIDTH), F32),
        pltpu.VMEM((n_seq, N_Q_HEADS * LANES), F32),
        pltpu.VMEM((n_seq, KV_WIDTH), F32),
        pltpu.VMEM((n_seq, KV_WIDTH), F32),
        pltpu.VMEM((n_seq, ATTN_WIDTH), F32),
        pltpu.VMEM((n_seq, D_MODEL), F32),
        pltpu.VMEM((n_seq, D_MODEL), F32),
        pltpu.VMEM((n_seq, POOL_WIDTH), BF16),
        pltpu.VMEM((n_seq, ATTN_WIDTH), BF16),
        pltpu.VMEM((BC * N_Q_HEADS, LANES), F32),
        pltpu.VMEM((BC * N_Q_HEADS, LANES), F32),
        pltpu.VMEM((BC * N_Q_HEADS, LANES), F32),
        pltpu.VMEM((BC * N_Q_HEADS, LANES), F32),
        pltpu.VMEM((SUBLANES, WINDOW), F32),
        pltpu.VMEM((POOL_GROUPS, POOL_GROUP_DIM, POOL_GROUP_DIM), BF16),
        pltpu.VMEM((POOL_WIDTH, D_MODEL), BF16),
        pltpu.VMEM((ATTN_WIDTH, D_MODEL), BF16),
        pltpu.VMEM((D_MODEL, D_MODEL), BF16),
    ]
    cache = pl.BlockSpec((BC, KV_WIDTH, WINDOW), lambda s: (seq_step(s), 0, 0))
    hist = pl.BlockSpec((POOL_STATE_ROWS, BC, POOL_WIDTH), lambda s: (0, seq_step(s), 0))
    return pl.pallas_call(
        _sample_kernel,
        grid=(W_STEPS + n_seq // BC,),
        in_specs=[_SMEM, _SMEM, _resident((n_seq, 1, D_MODEL)), cache, cache, hist,
                  _resident((1, D_MODEL)), _resident((1, POOL_WIDTH)), _resident((1, D_MODEL))] + weight_in_specs,
        out_specs=(pl.BlockSpec((n_seq, 1, D_MODEL), lambda s: (0, 0, 0)), cache, cache, hist) + weight_out_specs,
        out_shape=out_shape,
        scratch_shapes=scratch,
        compiler_params=pltpu.CompilerParams(
            dimension_semantics=("arbitrary",), vmem_limit_bytes=VMEM_LIMIT_BYTES),
        name="sample_layer",
    )(relb, sinks, x, cache_kt, cache_vt, state_t, g_norm, pool_scale, g_final, w_in, w_grp, w_brp, w_bra, w_out)


def _cache_as_kd_slot(cache):
    n = cache.shape[0]
    return cache.transpose(0, 2, 3, 1).reshape(n, KV_WIDTH, WINDOW)


def _cache_from_kd_slot(cache_t):
    n = cache_t.shape[0]
    return cache_t.reshape(n, N_KV_HEADS, HEAD_DIM, WINDOW).transpose(0, 3, 1, 2)


def kernel(x_prompt, x_sample, cache_k, cache_v, state_pool, rel_bias, g_norm, w_in, pool_w_grp, pool_scale, attn_sinks, w_br_pool, w_br_attn, w_out, g_final):
    depth = g_norm.shape[0]
    assert depth == 1 and x_sample.shape[1] == 1
    l = 0
    sinks = attn_sinks[l]
    relb = rel_bias.T
    g_in, p_scale, g_out = g_norm[l].reshape(1, D_MODEL), pool_scale[l].reshape(1, POOL_WIDTH), g_final.reshape(1, D_MODEL)

    y_s, k_s, v_s, pool_s, w_in_bf, w_grp_bf, w_brp_bf, w_bra_bf, w_out_bf = _sample_call(
        x_sample, _cache_as_kd_slot(cache_k[l]), _cache_as_kd_slot(cache_v[l]),
        state_pool[l].transpose(1, 0, 2), relb, sinks, g_in, p_scale, g_out,
        w_in, pool_w_grp, w_br_pool, w_br_attn, w_out)
    weights = (g_in, w_in_bf, w_grp_bf, p_scale, w_brp_bf, w_bra_bf, w_out_bf, g_out)
    y_p, k_p, v_p, pool_p = _prompt_call(x_prompt, relb, sinks, weights)
    return (
        y_p,
        y_s,
        _cache_from_kd_slot(k_p)[None],
        _cache_from_kd_slot(v_p)[None],
        pool_p[None],
        _cache_from_kd_slot(k_s)[None],
        _cache_from_kd_slot(v_s)[None],
        pool_s.transpose(1, 0, 2)[None],
    )
```

```python
import math

import jax
import jax.numpy as jnp
from jax import lax
from jax.experimental import pallas as pl
from jax.experimental.pallas import tpu as pltpu

D_MODEL = 1024
POOL_WINDOWS = (2, 4, 8, 16)
POOL_GROUPS = len(POOL_WINDOWS)
POOL_WIDTH = D_MODEL // 2
POOL_GROUP_DIM = POOL_WIDTH // POOL_GROUPS
POOL_STATE_ROWS = max(POOL_WINDOWS) - 1
HEAD_DIM = 64
N_KV_HEADS = 2
ATTN_WIDTH = D_MODEL // 2
N_Q_HEADS = ATTN_WIDTH // HEAD_DIM
GQA_GROUP = N_Q_HEADS // N_KV_HEADS
KV_WIDTH = N_KV_HEADS * HEAD_DIM
WINDOW = 128
N_BUCKETS = 32
MAX_DISTANCE = 128
RMS_EPS = 1e-6
IN_COLS = 2 * POOL_WIDTH + 2 * ATTN_WIDTH + 2 * KV_WIDTH + 2 * D_MODEL

OFF_U = 0
OFF_ZP = OFF_U + POOL_WIDTH
OFF_Q = OFF_ZP + POOL_WIDTH
OFF_K = OFF_Q + ATTN_WIDTH
OFF_V = OFF_K + KV_WIDTH
OFF_ZA = OFF_V + KV_WIDTH
OFF_GP = OFF_ZA + ATTN_WIDTH
OFF_GA = OFF_GP + D_MODEL

LANES = 128
SUBLANES = 8
POOL_HDR = 16
TM = 1024
BC = 32
W_STEPS = 4
VMEM_LIMIT_BYTES = 56 * 1024 * 1024

assert KV_WIDTH == LANES and 2 * HEAD_DIM == LANES and GQA_GROUP % 2 == 0

F32 = jnp.float32
BF16 = jnp.bfloat16
NEG_INF = float("-inf")


def _sigmoid(x):
    return 1.0 / (1.0 + jnp.exp(-x))


def _silu(x):
    return x * _sigmoid(x)


def _rmsnorm(x, g):
    return x * lax.rsqrt(jnp.mean(x * x, axis=-1, keepdims=True) + RMS_EPS) * g


def _dot(a, b):
    return jnp.dot(a, b, preferred_element_type=F32)


def _t5_bucket(rel):
    n = jnp.maximum(rel, 0)
    max_exact = N_BUCKETS // 2
    nf = jnp.maximum(n, 1).astype(F32)
    v = jnp.log(nf / max_exact) / math.log(MAX_DISTANCE / max_exact) * (N_BUCKETS - max_exact)
    steps = jnp.zeros(rel.shape, jnp.int32)
    for k in range(1, N_BUCKETS - max_exact):
        steps = steps + jnp.where(v >= k, 1, 0)
    return jnp.where(n < max_exact, n, max_exact + steps)


def _bias_table(rel, relb_ref, h):
    bucket = _t5_bucket(rel)

    def body(b, tbl):
        return jnp.where(bucket == b, relb_ref[h, b], tbl)

    tbl = lax.fori_loop(0, N_BUCKETS, body, jnp.zeros(rel.shape, F32))
    return jnp.where((rel >= 0) & (rel < WINDOW), tbl, NEG_INF)


def _per_head_column(ref_scalar):
    head = lax.broadcasted_iota(jnp.int32, (N_Q_HEADS, 1), 0)
    col = jnp.zeros((N_Q_HEADS, 1), F32)
    for h in range(N_Q_HEADS):
        col = jnp.where(head == h, ref_scalar(h), col)
    return col


def _project(xn_ref, w_in_ref, off, width):
    return _dot(xn_ref[...], w_in_ref[:, off:off + width])


def _kv_lane_mask(shape):
    return lax.broadcasted_iota(jnp.int32, shape, len(shape) - 1) < HEAD_DIM


def _head_pairs():
    for grp in range(ATTN_WIDTH // LANES):
        kv, pair = divmod(grp, GQA_GROUP // 2)
        yield grp, kv, kv * GQA_GROUP + 2 * pair, kv * GQA_GROUP + 2 * pair + 1


def _head_blocks(q):
    low = _kv_lane_mask((q.shape[0], LANES))
    blocks = [None] * N_Q_HEADS
    for grp, kv, even, odd in _head_pairs():
        a = q[:, grp * LANES:(grp + 1) * LANES]
        swapped = pltpu.roll(a, HEAD_DIM, 1)
        if kv == 0:
            blocks[even], blocks[odd] = jnp.where(low, a, 0.0), jnp.where(low, swapped, 0.0)
        else:
            blocks[even], blocks[odd] = jnp.where(low, 0.0, swapped), jnp.where(low, 0.0, a)
    return blocks


def _gather_heads(outs):
    low = _kv_lane_mask(outs[0].shape)
    groups = []
    for _, kv, even, odd in _head_pairs():
        if kv == 0:
            groups.append(jnp.where(low, outs[even], pltpu.roll(outs[odd], HEAD_DIM, 1)))
        else:
            groups.append(jnp.where(low, pltpu.roll(outs[even], HEAD_DIM, 1), outs[odd]))
    return jnp.concatenate(groups, axis=-1)


def _pool_windows(x_ext, inv_cnt):
    outs = []
    for g, w in enumerate(POOL_WINDOWS):
        xg = x_ext[:, g * POOL_GROUP_DIM:(g + 1) * POOL_GROUP_DIM]
        acc = xg
        shift = 1
        while shift < w:
            acc = acc + pltpu.roll(acc, shift, 0)
            shift *= 2
        outs.append(acc[POOL_HDR:] * inv_cnt[g] - xg[POOL_HDR:])
    return jnp.concatenate(outs, axis=-1)


def _residual_sum(x, pooled, zp_act, att_act, sig_gp, sig_ga, w_grp_ref, pool_scale, w_brp_ref, w_bra_ref,
                  w_out_ref):
    pg = jnp.concatenate(
        [_dot(pooled[:, g * POOL_GROUP_DIM:(g + 1) * POOL_GROUP_DIM], w_grp_ref[g])
         for g in range(POOL_GROUPS)], axis=-1)
    br_pool = _dot((pg * pool_scale * zp_act).astype(BF16), w_brp_ref[...])
    br_attn = _dot(att_act, w_bra_ref[...])
    merged = sig_gp * br_pool + sig_ga * br_attn
    return x + _dot(merged.astype(BF16), w_out_ref[...])


def _prompt_kernel(relb_ref, sinks_ref, x_ref, gnorm_ref, w_in_ref, w_grp_ref, pscale_ref,
                   w_brp_ref, w_bra_ref, w_out_ref, gfinal_ref,
                   y_ref, knew_ref, vnew_ref, pnew_ref,
                   xn_s, uext_s, zp_s, qh_s, kband_s, vband_s, za_s, gp_s, ga_s, pooled_s, att_s, bias_s):
    b = pl.program_id(0)
    j = pl.program_id(1)
    nblk = TM // WINDOW
    hrows = N_Q_HEADS * WINDOW

    @pl.when((b == 0) & (j == 0))
    def _build_bias():
        qi = lax.broadcasted_iota(jnp.int32, (WINDOW, 2 * WINDOW), 0)
        kc = lax.broadcasted_iota(jnp.int32, (WINDOW, 2 * WINDOW), 1)
        for h in range(N_Q_HEADS):
            bias_s[h * WINDOW:(h + 1) * WINDOW, :] = _bias_table(qi - kc + WINDOW, relb_ref, h)

    @pl.when(j == 0)
    def _reset_carry():
        uext_s[0:POOL_HDR, :] = jnp.zeros((POOL_HDR, POOL_WIDTH), F32)
        kband_s[0:WINDOW, :] = jnp.zeros((WINDOW, KV_WIDTH), BF16)
        vband_s[0:WINDOW, :] = jnp.zeros((WINDOW, KV_WIDTH), BF16)

    def _tile():
        xn_s[...] = _rmsnorm(x_ref[0], gnorm_ref[...]).astype(BF16)
        uext_s[POOL_HDR:, :] = _project(xn_s, w_in_ref, OFF_U, POOL_WIDTH)
        q_heads = _head_blocks(_project(xn_s, w_in_ref, OFF_Q, ATTN_WIDTH) * (HEAD_DIM ** -0.5))
        for c in range(nblk):
            for h in range(N_Q_HEADS):
                r0 = c * hrows + h * WINDOW
                qh_s[r0:r0 + WINDOW, :] = q_heads[h][c * WINDOW:(c + 1) * WINDOW].astype(BF16)

        pos = j * TM + lax.broadcasted_iota(jnp.int32, (TM, 1), 0)
        inv_cnt = [1.0 / jnp.minimum(pos + 1, w).astype(F32) for w in POOL_WINDOWS]
        x_ext = uext_s[...]
        pooled_s[...] = _pool_windows(x_ext, inv_cnt).astype(BF16)
        pnew_ref[0] = x_ext[POOL_HDR + TM - POOL_STATE_ROWS:, :]
        uext_s[0:POOL_HDR, :] = x_ext[TM:, :]

        kv = _project(xn_s, w_in_ref, OFF_K, 2 * KV_WIDTH)
        kband_s[WINDOW:, :] = kv[:, :KV_WIDTH].astype(BF16)
        vband_s[WINDOW:, :] = kv[:, KV_WIDTH:].astype(BF16)
        knew_ref[0] = kv[TM - WINDOW:, :KV_WIDTH].T
        vnew_ref[0] = kv[TM - WINDOW:, KV_WIDTH:].T
        za_s[...] = _silu(_project(xn_s, w_in_ref, OFF_ZA, ATTN_WIDTH))

        first_neg = jnp.where(j == 0, NEG_INF, 0.0).astype(F32)
        for c in range(nblk):
            rows = slice(c * WINDOW, (c + 1) * WINDOW)
            k_c = kband_s[c * WINDOW:(c + 2) * WINDOW, :]
            v_c = vband_s[c * WINDOW:(c + 2) * WINDOW, :]
            s_all = lax.dot_general(qh_s[c * hrows:(c + 1) * hrows, :], k_c, (((1,), (1,)), ((), ())),
                                    preferred_element_type=F32)
            p_parts, inv_parts = [], []
            for h in range(N_Q_HEADS):
                head_rows = slice(h * WINDOW, (h + 1) * WINDOW)
                sc = s_all[head_rows, :] + bias_s[head_rows, :]
                if c == 0:
                    sc = jnp.concatenate([sc[:, :WINDOW] + first_neg, sc[:, WINDOW:]], axis=-1)
                sink = sinks_ref[h]
                m = jnp.maximum(jnp.max(sc, axis=-1, keepdims=True), sink)
                p = jnp.exp(sc - m)
                denom = jnp.sum(p, axis=-1, keepdims=True) + jnp.exp(sink - m)
                p_parts.append(p.astype(BF16))
                inv_parts.append(1.0 / denom)
            o_all = _dot(jnp.concatenate(p_parts, axis=0), v_c)
            outs = [o_all[h * WINDOW:(h + 1) * WINDOW, :] * inv_parts[h] for h in range(N_Q_HEADS)]
            att_s[rows, :] = (_gather_heads(outs) * za_s[rows, :]).astype(BF16)
        kband_s[0:WINDOW, :] = kband_s[TM:, :]
        vband_s[0:WINDOW, :] = vband_s[TM:, :]

        zp_s[...] = _silu(_project(xn_s, w_in_ref, OFF_ZP, POOL_WIDTH))
        for half in range(2):
            cols = slice(half * POOL_WIDTH, (half + 1) * POOL_WIDTH)
            gp_s[:, cols] = _sigmoid(_project(xn_s, w_in_ref, OFF_GP + half * POOL_WIDTH, POOL_WIDTH))
            ga_s[:, cols] = _sigmoid(_project(xn_s, w_in_ref, OFF_GA + half * POOL_WIDTH, POOL_WIDTH))

        out = _residual_sum(x_ref[0], pooled_s[...], zp_s[...], att_s[...], gp_s[...], ga_s[...],
                            w_grp_ref, pscale_ref[...], w_brp_ref, w_bra_ref, w_out_ref)
        y_ref[0] = _rmsnorm(out, gfinal_ref[...])

    _tile()


def _resident(shape):
    return pl.BlockSpec(shape, lambda *_: (0,) * len(shape), pipeline_mode=pl.Buffered(1))


_SMEM = pl.BlockSpec(memory_space=pltpu.SMEM)


def _weight_specs():
    return [
        _resident((1, D_MODEL)),
        _resident((D_MODEL, IN_COLS)),
        _resident((POOL_GROUPS, POOL_GROUP_DIM, POOL_GROUP_DIM)),
        _resident((1, POOL_WIDTH)),
        _resident((POOL_WIDTH, D_MODEL)),
        _resident((ATTN_WIDTH, D_MODEL)),
        _resident((D_MODEL, D_MODEL)),
        _resident((1, D_MODEL)),
    ]


def _prompt_call(x, relb, sinks, weights):
    batch, seq, _ = x.shape
    assert seq % TM == 0 and TM % WINDOW == 0 and TM >= POOL_HDR
    tile_block = pl.BlockSpec((1, TM, D_MODEL), lambda b, j: (b, j, 0))

    def seq_block(shape):
        return pl.BlockSpec((1,) + shape, lambda b, j: (b, 0, 0))

    out_shape = (
        jax.ShapeDtypeStruct((batch, seq, D_MODEL), F32),
        jax.ShapeDtypeStruct((batch, KV_WIDTH, WINDOW), F32),
        jax.ShapeDtypeStruct((batch, KV_WIDTH, WINDOW), F32),
        jax.ShapeDtypeStruct((batch, POOL_STATE_ROWS, POOL_WIDTH), F32),
    )
    scratch = [
        pltpu.VMEM((TM, D_MODEL), BF16),
        pltpu.VMEM((POOL_HDR + TM, POOL_WIDTH), F32),
        pltpu.VMEM((TM, POOL_WIDTH), F32),
        pltpu.VMEM((TM * N_Q_HEADS, LANES), BF16),
        pltpu.VMEM((WINDOW + TM, KV_WIDTH), BF16),
        pltpu.VMEM((WINDOW + TM, KV_WIDTH), BF16),
        pltpu.VMEM((TM, ATTN_WIDTH), F32),
        pltpu.VMEM((TM, D_MODEL), F32),
        pltpu.VMEM((TM, D_MODEL), F32),
        pltpu.VMEM((TM, POOL_WIDTH), BF16),
        pltpu.VMEM((TM, ATTN_WIDTH), BF16),
        pltpu.VMEM((N_Q_HEADS * WINDOW, 2 * WINDOW), F32),
    ]
    return pl.pallas_call(
        _prompt_kernel,
        grid=(batch, seq // TM),
        in_specs=[_SMEM, _SMEM, tile_block] + _weight_specs(),
        out_specs=(
            tile_block,
            seq_block((KV_WIDTH, WINDOW)),
            seq_block((KV_WIDTH, WINDOW)),
            seq_block((POOL_STATE_ROWS, POOL_WIDTH)),
        ),
        out_shape=out_shape,
        scratch_shapes=scratch,
        compiler_params=pltpu.CompilerParams(
            dimension_semantics=("arbitrary", "arbitrary"), vmem_limit_bytes=VMEM_LIMIT_BYTES),
        name="prompt_layer",
    )(relb, sinks, x, *weights)


def _last_slot_columns(rows, place_ref):
    t = rows.T
    hi = t.astype(BF16)
    rest = t - hi.astype(F32)
    mid = rest.astype(BF16)
    lo = (rest - mid.astype(F32)).astype(BF16)
    return _dot(jnp.concatenate([hi, mid, lo], axis=1), place_ref[...])


def _placement_matrix():
    r = lax.broadcasted_iota(jnp.int32, (3 * BC, BC * WINDOW), 0)
    c = lax.broadcasted_iota(jnp.int32, (3 * BC, BC * WINDOW), 1)
    hit = (c == (r % BC) * WINDOW + WINDOW - 1)
    return jnp.where(hit, 1.0, 0.0).astype(BF16)


def _sample_kernel(relb_ref, sinks_ref, x_ref, kt_ref, vt_ref, st_ref, gnorm_ref, pscale_ref, gfinal_ref,
                   w_in_ref, w_grp_ref, w_brp_ref, w_bra_ref, w_out_ref,
                   y_ref, knew_ref, vnew_ref, pnew_ref, o_in_ref, o_grp_ref, o_brp_ref, o_bra_ref, o_out_ref,
                   xn_s, h_s, u_s, zp_s, qh_s, k_s, v_s, za_s, gp_s, ga_s, pooled_s, att_s,
                   q2_s, kn2_s, vn2_s, o2_s, bias_s, place_s, wgrp_s, wbrp_s, wbra_s, wout_s):
    s = pl.program_id(0)
    n_steps = pl.num_programs(0)
    n_seq = x_ref.shape[0]
    rows_in = D_MODEL // W_STEPS

    @pl.when(s == 0)
    def _normalise():
        xn = _rmsnorm(x_ref[:, 0, :], gnorm_ref[...]).astype(BF16)
        for c in range(W_STEPS):
            xn_s[c] = xn[:, c * rows_in:(c + 1) * rows_in]
        h_s[...] = jnp.zeros((n_seq, IN_COLS), F32)

    @pl.when(s < W_STEPS)
    def _weight_chunk():
        w_in_bf = w_in_ref[0].astype(BF16)
        o_in_ref[...] = w_in_bf
        h_s[...] += _dot(xn_s[s], w_in_bf)
        for w_ref, o_ref, copy_s, axis in ((w_grp_ref, o_grp_ref, wgrp_s, 1), (w_brp_ref, o_brp_ref, wbrp_s, 0),
                                          (w_bra_ref, o_bra_ref, wbra_s, 0), (w_out_ref, o_out_ref, wout_s, 0)):
            w_bf = w_ref[0].astype(BF16)
            o_ref[...] = w_bf
            rows = w_bf.shape[axis]
            chunk_rows = pl.ds(pl.multiple_of(s * rows, rows), rows)
            if axis == 0:
                copy_s[chunk_rows, :] = w_bf
            else:
                copy_s[:, chunk_rows, :] = w_bf

    @pl.when(s == W_STEPS - 1)
    def _activations():
        slot = lax.broadcasted_iota(jnp.int32, (SUBLANES, WINDOW), 1)
        head = lax.broadcasted_iota(jnp.int32, (SUBLANES, WINDOW), 0)
        tbl = jnp.zeros((SUBLANES, WINDOW), F32)
        for h in range(N_Q_HEADS):
            tbl = jnp.where(head == h, _bias_table(WINDOW - slot, relb_ref, h), tbl)
        bias_s[...] = tbl
        place_s[...] = _placement_matrix()

        u_s[...] = h_s[:, OFF_U:OFF_U + POOL_WIDTH]
        zp_s[...] = _silu(h_s[:, OFF_ZP:OFF_ZP + POOL_WIDTH])
        q_heads = _head_blocks(h_s[:, OFF_Q:OFF_Q + ATTN_WIDTH] * (HEAD_DIM ** -0.5))
        for h in range(N_Q_HEADS):
            qh_s[:, h * LANES:(h + 1) * LANES] = q_heads[h]
        k_s[...] = h_s[:, OFF_K:OFF_K + KV_WIDTH]
        v_s[...] = h_s[:, OFF_V:OFF_V + KV_WIDTH]
        za_s[...] = _silu(h_s[:, OFF_ZA:OFF_ZA + ATTN_WIDTH])
        gp_s[...] = _sigmoid(h_s[:, OFF_GP:OFF_GP + D_MODEL])
        ga_s[...] = _sigmoid(h_s[:, OFF_GA:OFF_GA + D_MODEL])

    @pl.when(s >= W_STEPS)
    def _sequence_chunk():
        i = s - W_STEPS
        chunk = pl.ds(pl.multiple_of(i * BC, BC), BC)

        u_new = u_s[chunk, :]
        n_hist = POOL_STATE_ROWS
        parts = []
        for g, w in enumerate(POOL_WINDOWS):
            cols = slice(g * POOL_GROUP_DIM, (g + 1) * POOL_GROUP_DIM)
            acc = u_new[:, cols]
            for r in range(n_hist - (w - 1), n_hist):
                acc = acc + st_ref[r, :, cols]
            parts.append(acc * (1.0 / w) - u_new[:, cols])
        pooled_s[chunk, :] = jnp.concatenate(parts, axis=-1).astype(BF16)
        pnew_ref[0:n_hist - 1] = st_ref[1:n_hist]
        pnew_ref[n_hist - 1] = u_new

        k_new = k_s[chunk, :]
        v_new = v_s[chunk, :]
        last_slot = lax.broadcasted_iota(jnp.int32, (KV_WIDTH, WINDOW), 1) == WINDOW - 1
        k_cols = _last_slot_columns(k_new, place_s)
        v_cols = _last_slot_columns(v_new, place_s)
        for b in range(BC):
            lanes = slice(b * WINDOW, (b + 1) * WINDOW)
            knew_ref[b] = jnp.where(last_slot, k_cols[:, lanes], pltpu.roll(kt_ref[b], WINDOW - 1, 1))
            vnew_ref[b] = jnp.where(last_slot, v_cols[:, lanes], pltpu.roll(vt_ref[b], WINDOW - 1, 1))

        for h in range(N_Q_HEADS):
            q2_s[pl.ds(h, BC, stride=N_Q_HEADS), :] = qh_s[chunk, h * LANES:(h + 1) * LANES]
            kn2_s[pl.ds(h, BC, stride=N_Q_HEADS), :] = k_new
            vn2_s[pl.ds(h, BC, stride=N_Q_HEADS), :] = v_new
        q3 = q2_s[...].reshape(BC, N_Q_HEADS, LANES)
        kn3 = kn2_s[...].reshape(BC, N_Q_HEADS, LANES)
        vn3 = vn2_s[...].reshape(BC, N_Q_HEADS, LANES)
        s3 = lax.dot_general(q3.astype(BF16), kt_ref[...].astype(BF16), (((2,), (1,)), ((0,), (0,))),
                             preferred_element_type=F32) + bias_s[...][None]
        s_self = jnp.sum(q3 * kn3, axis=-1, keepdims=True) + _per_head_column(lambda h: relb_ref[h, 0])[None]
        sink = _per_head_column(lambda h: sinks_ref[h])[None]
        m = jnp.maximum(jnp.maximum(jnp.max(s3, axis=-1, keepdims=True), s_self), sink)
        p3 = jnp.exp(s3 - m)
        p_self = jnp.exp(s_self - m)
        denom = jnp.sum(p3, axis=-1, keepdims=True) + p_self + jnp.exp(sink - m)
        o3 = lax.dot_general(p3.astype(BF16), vt_ref[...].astype(BF16), (((2,), (2,)), ((0,), (0,))),
                             preferred_element_type=F32)
        o3 = (o3 + p_self * vn3) * (1.0 / denom)
        o2_s[...] = o3.reshape(BC * N_Q_HEADS, LANES)
        outs = [o2_s[pl.ds(h, BC, stride=N_Q_HEADS), :] for h in range(N_Q_HEADS)]
        att_s[chunk, :] = (_gather_heads(outs) * za_s[chunk, :]).astype(BF16)

    @pl.when(s == n_steps - 1)
    def _merge_all():
        out = _residual_sum(x_ref[:, 0, :], pooled_s[...], zp_s[...], att_s[...], gp_s[...], ga_s[...],
                            wgrp_s, pscale_ref[...], wbrp_s, wbra_s, wout_s)
        y_ref[:, 0, :] = _rmsnorm(out, gfinal_ref[...])


def _sample_call(x, cache_kt, cache_vt, state_t, relb, sinks, g_norm, pool_scale, g_final,
                 w_in, w_grp, w_brp, w_bra, w_out):
    n_seq = x.shape[0]
    assert n_seq % BC == 0

    def chunk_rows(n):
        assert n % (W_STEPS * 2 * SUBLANES) == 0
        return n // W_STEPS

    r_in, r_grp, r_br, r_out = (chunk_rows(D_MODEL), chunk_rows(POOL_GROUP_DIM), chunk_rows(POOL_WIDTH),
                                chunk_rows(D_MODEL))

    def w_step(s):
        return jnp.minimum(s, W_STEPS - 1)

    def seq_step(s):
        return jnp.maximum(s - W_STEPS, 0)

    weight_in_specs = [
        pl.BlockSpec((1, r_in, IN_COLS), lambda s: (0, w_step(s), 0)),
        pl.BlockSpec((1, POOL_GROUPS, r_grp, POOL_GROUP_DIM), lambda s: (0, 0, w_step(s), 0)),
        pl.BlockSpec((1, r_br, D_MODEL), lambda s: (0, w_step(s), 0)),
        pl.BlockSpec((1, r_br, D_MODEL), lambda s: (0, w_step(s), 0)),
        pl.BlockSpec((1, r_out, D_MODEL), lambda s: (0, w_step(s), 0)),
    ]
    weight_out_specs = (
        pl.BlockSpec((r_in, IN_COLS), lambda s: (w_step(s), 0)),
        pl.BlockSpec((POOL_GROUPS, r_grp, POOL_GROUP_DIM), lambda s: (0, w_step(s), 0)),
        pl.BlockSpec((r_br, D_MODEL), lambda s: (w_step(s), 0)),
        pl.BlockSpec((r_br, D_MODEL), lambda s: (w_step(s), 0)),
        pl.BlockSpec((r_out, D_MODEL), lambda s: (w_step(s), 0)),
    )
    out_shape = (
        jax.ShapeDtypeStruct((n_seq, 1, D_MODEL), F32),
        jax.ShapeDtypeStruct((n_seq, KV_WIDTH, WINDOW), F32),
        jax.ShapeDtypeStruct((n_seq, KV_WIDTH, WINDOW), F32),
        jax.ShapeDtypeStruct((POOL_STATE_ROWS, n_seq, POOL_WIDTH), F32),
        jax.ShapeDtypeStruct((D_MODEL, IN_COLS), BF16),
        jax.ShapeDtypeStruct((POOL_GROUPS, POOL_GROUP_DIM, POOL_GROUP_DIM), BF16),
        jax.ShapeDtypeStruct((POOL_WIDTH, D_MODEL), BF16),
        jax.ShapeDtypeStruct((ATTN_WIDTH, D_MODEL), BF16),
        jax.ShapeDtypeStruct((D_MODEL, D_MODEL), BF16),
    )
    scratch = [
        pltpu.VMEM((W_STEPS, n_seq, D_MODEL // W_STEPS), BF16),
        pltpu.VMEM((n_seq, IN_COLS), F32),
        pltpu.VMEM((n_seq, POOL_WIDTH), F32),
        pltpu.VMEM((n_seq, POOL_WIDTH), F32),
        pltpu.VMEM((n_seq, N_Q_HEADS * LANES), F32),
        pltpu.VMEM((n_seq, KV_WIDTH), F32),
        pltpu.VMEM((n_seq, KV_WIDTH), F32),
        pltpu.VMEM((n_seq, ATTN_WIDTH), F32),
        pltpu.VMEM((n_seq, D_MODEL), F32),
        pltpu.VMEM((n_seq, D_MODEL), F32),
        pltpu.VMEM((n_seq, POOL_WIDTH), BF16),
        pltpu.VMEM((n_seq, ATTN_WIDTH), BF16),
        pltpu.VMEM((BC * N_Q_HEADS, LANES), F32),
        pltpu.VMEM((BC * N_Q_HEADS, LANES), F32),
        pltpu.VMEM((BC * N_Q_HEADS, LANES), F32),
        pltpu.VMEM((BC * N_Q_HEADS, LANES), F32),
        pltpu.VMEM((SUBLANES, WINDOW), F32),
        pltpu.VMEM((3 * BC, BC * WINDOW), BF16),
        pltpu.VMEM((POOL_GROUPS, POOL_GROUP_DIM, POOL_GROUP_DIM), BF16),
        pltpu.VMEM((POOL_WIDTH, D_MODEL), BF16),
        pltpu.VMEM((ATTN_WIDTH, D_MODEL), BF16),
        pltpu.VMEM((D_MODEL, D_MODEL), BF16),
    ]
    cache = pl.BlockSpec((BC, KV_WIDTH, WINDOW), lambda s: (seq_step(s), 0, 0))
    hist = pl.BlockSpec((POOL_STATE_ROWS, BC, POOL_WIDTH), lambda s: (0, seq_step(s), 0))
    return pl.pallas_call(
        _sample_kernel,
        grid=(W_STEPS + n_seq // BC,),
        in_specs=[_SMEM, _SMEM, _resident((n_seq, 1, D_MODEL)), cache, cache, hist,
                  _resident((1, D_MODEL)), _resident((1, POOL_WIDTH)), _resident((1, D_MODEL))] + weight_in_specs,
        out_specs=(pl.BlockSpec((n_seq, 1, D_MODEL), lambda s: (0, 0, 0)), cache, cache, hist) + weight_out_specs,
        out_shape=out_shape,
        scratch_shapes=scratch,
        compiler_params=pltpu.CompilerParams(
            dimension_semantics=("arbitrary",), vmem_limit_bytes=VMEM_LIMIT_BYTES),
        name="sample_layer",
    )(relb, sinks, x, cache_kt, cache_vt, state_t, g_norm, pool_scale, g_final, w_in, w_grp, w_brp, w_bra, w_out)


def _cache_as_kd_slot(cache):
    n = cache.shape[0]
    return cache.transpose(0, 2, 3, 1).reshape(n, KV_WIDTH, WINDOW)


def _cache_from_kd_slot(cache_t):
    n = cache_t.shape[0]
    return cache_t.reshape(n, N_KV_HEADS, HEAD_DIM, WINDOW).transpose(0, 3, 1, 2)


def kernel(x_prompt, x_sample, cache_k, cache_v, state_pool, rel_bias, g_norm, w_in, pool_w_grp, pool_scale, attn_sinks, w_br_pool, w_br_attn, w_out, g_final):
    depth = g_norm.shape[0]
    assert depth == 1 and x_sample.shape[1] == 1
    l = 0
    sinks = attn_sinks[l]
    relb = rel_bias.T
    g_in, p_scale, g_out = g_norm[l].reshape(1, D_MODEL), pool_scale[l].reshape(1, POOL_WIDTH), g_final.reshape(1, D_MODEL)

    y_s, k_s, v_s, pool_s, w_in_bf, w_grp_bf, w_brp_bf, w_bra_bf, w_out_bf = _sample_call(
        x_sample, _cache_as_kd_slot(cache_k[l]), _cache_as_kd_slot(cache_v[l]),
        state_pool[l].transpose(1, 0, 2), relb, sinks, g_in, p_scale, g_out,
        w_in, pool_w_grp, w_br_pool, w_br_attn, w_out)
    weights = (g_in, w_in_bf, w_grp_bf, p_scale, w_brp_bf, w_bra_bf, w_out_bf, g_out)
    y_p, k_p, v_p, pool_p = _prompt_call(x_prompt, relb, sinks, weights)
    return (
        y_p,
        y_s,
        _cache_from_kd_slot(k_p)[None],
        _cache_from_kd_slot(v_p)[None],
        pool_p[None],
        _cache_from_kd_slot(k_s)[None],
        _cache_from_kd_slot(v_s)[None],
        pool_s.transpose(1, 0, 2)[None],
    )
```

```python
import math

import jax
import jax.numpy as jnp
from jax import lax
from jax.experimental import pallas as pl
from jax.experimental.pallas import tpu as pltpu

D_MODEL = 1024
POOL_WINDOWS = (2, 4, 8, 16)
POOL_GROUPS = len(POOL_WINDOWS)
POOL_WIDTH = D_MODEL // 2
POOL_GROUP_DIM = POOL_WIDTH // POOL_GROUPS
POOL_STATE_ROWS = max(POOL_WINDOWS) - 1
HEAD_DIM = 64
N_KV_HEADS = 2
ATTN_WIDTH = D_MODEL // 2
N_Q_HEADS = ATTN_WIDTH // HEAD_DIM
GQA_GROUP = N_Q_HEADS // N_KV_HEADS
KV_WIDTH = N_KV_HEADS * HEAD_DIM
WINDOW = 128
N_BUCKETS = 32
MAX_DISTANCE = 128
RMS_EPS = 1e-6
IN_COLS = 2 * POOL_WIDTH + 2 * ATTN_WIDTH + 2 * KV_WIDTH + 2 * D_MODEL

OFF_U = 0
OFF_ZP = OFF_U + POOL_WIDTH
OFF_Q = OFF_ZP + POOL_WIDTH
OFF_K = OFF_Q + ATTN_WIDTH
OFF_V = OFF_K + KV_WIDTH
OFF_ZA = OFF_V + KV_WIDTH
OFF_GP = OFF_ZA + ATTN_WIDTH
OFF_GA = OFF_GP + D_MODEL

LANES = 128
SUBLANES = 8
POOL_HDR = 16
TM = 1024
BC = 32
W_STEPS = 4
VMEM_LIMIT_BYTES = 56 * 1024 * 1024

assert KV_WIDTH == LANES and 2 * HEAD_DIM == LANES and GQA_GROUP % 2 == 0

F32 = jnp.float32
BF16 = jnp.bfloat16
NEG_INF = float("-inf")


def _sigmoid(x):
    return 1.0 / (1.0 + jnp.exp(-x))


def _silu(x):
    return x * _sigmoid(x)


def _rmsnorm(x, g):
    return x * lax.rsqrt(jnp.mean(x * x, axis=-1, keepdims=True) + RMS_EPS) * g


def _dot(a, b):
    return jnp.dot(a, b, preferred_element_type=F32)


def _t5_bucket(rel):
    n = jnp.maximum(rel, 0)
    max_exact = N_BUCKETS // 2
    nf = jnp.maximum(n, 1).astype(F32)
    v = jnp.log(nf / max_exact) / math.log(MAX_DISTANCE / max_exact) * (N_BUCKETS - max_exact)
    steps = jnp.zeros(rel.shape, jnp.int32)
    for k in range(1, N_BUCKETS - max_exact):
        steps = steps + jnp.where(v >= k, 1, 0)
    return jnp.where(n < max_exact, n, max_exact + steps)


def _bias_table(rel, relb_ref, h):
    bucket = _t5_bucket(rel)

    def body(b, tbl):
        return jnp.where(bucket == b, relb_ref[h, b], tbl)

    tbl = lax.fori_loop(0, N_BUCKETS, body, jnp.zeros(rel.shape, F32))
    return jnp.where((rel >= 0) & (rel < WINDOW), tbl, NEG_INF)


def _per_head_column(ref_scalar):
    head = lax.broadcasted_iota(jnp.int32, (N_Q_HEADS, 1), 0)
    col = jnp.zeros((N_Q_HEADS, 1), F32)
    for h in range(N_Q_HEADS):
        col = jnp.where(head == h, ref_scalar(h), col)
    return col


def _project(xn_ref, w_in_ref, off, width):
    return _dot(xn_ref[...], w_in_ref[:, off:off + width])


def _kv_lane_mask(shape):
    return lax.broadcasted_iota(jnp.int32, shape, len(shape) - 1) < HEAD_DIM


def _head_pairs():
    for grp in range(ATTN_WIDTH // LANES):
        kv, pair = divmod(grp, GQA_GROUP // 2)
        yield grp, kv, kv * GQA_GROUP + 2 * pair, kv * GQA_GROUP + 2 * pair + 1


def _head_blocks(q):
    low = _kv_lane_mask((q.shape[0], LANES))
    blocks = [None] * N_Q_HEADS
    for grp, kv, even, odd in _head_pairs():
        a = q[:, grp * LANES:(grp + 1) * LANES]
        swapped = pltpu.roll(a, HEAD_DIM, 1)
        if kv == 0:
            blocks[even], blocks[odd] = jnp.where(low, a, 0.0), jnp.where(low, swapped, 0.0)
        else:
            blocks[even], blocks[odd] = jnp.where(low, 0.0, swapped), jnp.where(low, 0.0, a)
    return blocks


def _gather_heads(outs):
    low = _kv_lane_mask(outs[0].shape)
    groups = []
    for _, kv, even, odd in _head_pairs():
        if kv == 0:
            groups.append(jnp.where(low, outs[even], pltpu.roll(outs[odd], HEAD_DIM, 1)))
        else:
            groups.append(jnp.where(low, pltpu.roll(outs[even], HEAD_DIM, 1), outs[odd]))
    return jnp.concatenate(groups, axis=-1)


def _pool_windows(x_ext, inv_cnt):
    outs = []
    for g, w in enumerate(POOL_WINDOWS):
        xg = x_ext[:, g * POOL_GROUP_DIM:(g + 1) * POOL_GROUP_DIM]
        acc = xg
        shift = 1
        while shift < w:
            acc = acc + pltpu.roll(acc, shift, 0)
            shift *= 2
        outs.append(acc[POOL_HDR:] * inv_cnt[g] - xg[POOL_HDR:])
    return jnp.concatenate(outs, axis=-1)


def _residual_sum(x, pooled, zp_act, att_act, sig_gp, sig_ga, w_grp_ref, pool_scale, w_brp_ref, w_bra_ref,
                  w_out_ref):
    pg = jnp.concatenate(
        [_dot(pooled[:, g * POOL_GROUP_DIM:(g + 1) * POOL_GROUP_DIM], w_grp_ref[g])
         for g in range(POOL_GROUPS)], axis=-1)
    br_pool = _dot((pg * pool_scale * zp_act).astype(BF16), w_brp_ref[...])
    br_attn = _dot(att_act, w_bra_ref[...])
    merged = sig_gp * br_pool + sig_ga * br_attn
    return x + _dot(merged.astype(BF16), w_out_ref[...])


def _prompt_kernel(relb_ref, sinks_ref, x_ref, gnorm_ref, w_in_ref, w_grp_ref, pscale_ref,
                   w_brp_ref, w_bra_ref, w_out_ref, gfinal_ref,
                   y_ref, knew_ref, vnew_ref, pnew_ref,
                   xn_s, uext_s, zp_s, qh_s, kband_s, vband_s, za_s, gp_s, ga_s, pooled_s, att_s, bias_s):
    b = pl.program_id(0)
    j = pl.program_id(1)
    nblk = TM // WINDOW
    hrows = N_Q_HEADS * WINDOW

    @pl.when((b == 0) & (j == 0))
    def _build_bias():
        qi = lax.broadcasted_iota(jnp.int32, (WINDOW, 2 * WINDOW), 0)
        kc = lax.broadcasted_iota(jnp.int32, (WINDOW, 2 * WINDOW), 1)
        for h in range(N_Q_HEADS):
            bias_s[h * WINDOW:(h + 1) * WINDOW, :] = _bias_table(qi - kc + WINDOW, relb_ref, h)

    @pl.when(j == 0)
    def _reset_carry():
        uext_s[0:POOL_HDR, :] = jnp.zeros((POOL_HDR, POOL_WIDTH), F32)
        kband_s[0:WINDOW, :] = jnp.zeros((WINDOW, KV_WIDTH), BF16)
        vband_s[0:WINDOW, :] = jnp.zeros((WINDOW, KV_WIDTH), BF16)

    def _tile():
        xn_s[...] = _rmsnorm(x_ref[0], gnorm_ref[...]).astype(BF16)
        uext_s[POOL_HDR:, :] = _project(xn_s, w_in_ref, OFF_U, POOL_WIDTH)
        q_heads = _head_blocks(_project(xn_s, w_in_ref, OFF_Q, ATTN_WIDTH) * (HEAD_DIM ** -0.5))
        for c in range(nblk):
            for h in range(N_Q_HEADS):
                r0 = c * hrows + h * WINDOW
                qh_s[r0:r0 + WINDOW, :] = q_heads[h][c * WINDOW:(c + 1) * WINDOW].astype(BF16)

        pos = j * TM + lax.broadcasted_iota(jnp.int32, (TM, 1), 0)
        inv_cnt = [1.0 / jnp.minimum(pos + 1, w).astype(F32) for w in POOL_WINDOWS]
        x_ext = uext_s[...]
        pooled_s[...] = _pool_windows(x_ext, inv_cnt).astype(BF16)
        pnew_ref[:, pl.ds(b, 1), :] = x_ext[POOL_HDR + TM - POOL_STATE_ROWS:, :][:, None, :]
        uext_s[0:POOL_HDR, :] = x_ext[TM:, :]

        kv = _project(xn_s, w_in_ref, OFF_K, 2 * KV_WIDTH)
        kband_s[WINDOW:, :] = kv[:, :KV_WIDTH].astype(BF16)
        vband_s[WINDOW:, :] = kv[:, KV_WIDTH:].astype(BF16)
        knew_ref[0] = kv[TM - WINDOW:, :KV_WIDTH].T
        vnew_ref[0] = kv[TM - WINDOW:, KV_WIDTH:].T
        za_s[...] = _silu(_project(xn_s, w_in_ref, OFF_ZA, ATTN_WIDTH))

        first_neg = jnp.where(j == 0, NEG_INF, 0.0).astype(F32)
        for c in range(nblk):
            rows = slice(c * WINDOW, (c + 1) * WINDOW)
            k_c = kband_s[c * WINDOW:(c + 2) * WINDOW, :]
            v_c = vband_s[c * WINDOW:(c + 2) * WINDOW, :]
            s_all = lax.dot_general(qh_s[c * hrows:(c + 1) * hrows, :], k_c, (((1,), (1,)), ((), ())),
                                    preferred_element_type=F32)
            p_parts, inv_parts = [], []
            for h in range(N_Q_HEADS):
                head_rows = slice(h * WINDOW, (h + 1) * WINDOW)
                sc = s_all[head_rows, :] + bias_s[head_rows, :]
                if c == 0:
                    sc = jnp.concatenate([sc[:, :WINDOW] + first_neg, sc[:, WINDOW:]], axis=-1)
                sink = sinks_ref[h]
                m = jnp.maximum(jnp.max(sc, axis=-1, keepdims=True), sink)
                p = jnp.exp(sc - m)
                denom = jnp.sum(p, axis=-1, keepdims=True) + jnp.exp(sink - m)
                p_parts.append(p.astype(BF16))
                inv_parts.append(1.0 / denom)
            o_all = _dot(jnp.concatenate(p_parts, axis=0), v_c)
            outs = [o_all[h * WINDOW:(h + 1) * WINDOW, :] * inv_parts[h] for h in range(N_Q_HEADS)]
            att_s[rows, :] = (_gather_heads(outs) * za_s[rows, :]).astype(BF16)
        kband_s[0:WINDOW, :] = kband_s[TM:, :]
        vband_s[0:WINDOW, :] = vband_s[TM:, :]

        zp_s[...] = _silu(_project(xn_s, w_in_ref, OFF_ZP, POOL_WIDTH))
        for half in range(2):
            cols = slice(half * POOL_WIDTH, (half + 1) * POOL_WIDTH)
            gp_s[:, cols] = _sigmoid(_project(xn_s, w_in_ref, OFF_GP + half * POOL_WIDTH, POOL_WIDTH))
            ga_s[:, cols] = _sigmoid(_project(xn_s, w_in_ref, OFF_GA + half * POOL_WIDTH, POOL_WIDTH))

        out = _residual_sum(x_ref[0], pooled_s[...], zp_s[...], att_s[...], gp_s[...], ga_s[...],
                            w_grp_ref, pscale_ref[...], w_brp_ref, w_bra_ref, w_out_ref)
        y_ref[0] = _rmsnorm(out, gfinal_ref[...])

    _tile()


def _resident(shape):
    return pl.BlockSpec(shape, lambda *_: (0,) * len(shape), pipeline_mode=pl.Buffered(1))


_SMEM = pl.BlockSpec(memory_space=pltpu.SMEM)


def _weight_specs():
    return [
        _resident((1, D_MODEL)),
        _resident((D_MODEL, IN_COLS)),
        _resident((POOL_GROUPS, POOL_GROUP_DIM, POOL_GROUP_DIM)),
        _resident((1, POOL_WIDTH)),
        _resident((POOL_WIDTH, D_MODEL)),
        _resident((ATTN_WIDTH, D_MODEL)),
        _resident((D_MODEL, D_MODEL)),
        _resident((1, D_MODEL)),
    ]


def _prompt_call(x, relb, sinks, weights):
    batch, seq, _ = x.shape
    assert seq % TM == 0 and TM % WINDOW == 0 and TM >= POOL_HDR
    tile_block = pl.BlockSpec((1, TM, D_MODEL), lambda b, j: (b, j, 0))

    def seq_block(shape):
        return pl.BlockSpec((1,) + shape, lambda b, j: (b, 0, 0))

    out_shape = (
        jax.ShapeDtypeStruct((batch, seq, D_MODEL), F32),
        jax.ShapeDtypeStruct((batch, KV_WIDTH, WINDOW), F32),
        jax.ShapeDtypeStruct((batch, KV_WIDTH, WINDOW), F32),
        jax.ShapeDtypeStruct((POOL_STATE_ROWS, batch, POOL_WIDTH), F32),
    )
    scratch = [
        pltpu.VMEM((TM, D_MODEL), BF16),
        pltpu.VMEM((POOL_HDR + TM, POOL_WIDTH), F32),
        pltpu.VMEM((TM, POOL_WIDTH), F32),
        pltpu.VMEM((TM * N_Q_HEADS, LANES), BF16),
        pltpu.VMEM((WINDOW + TM, KV_WIDTH), BF16),
        pltpu.VMEM((WINDOW + TM, KV_WIDTH), BF16),
        pltpu.VMEM((TM, ATTN_WIDTH), F32),
        pltpu.VMEM((TM, D_MODEL), F32),
        pltpu.VMEM((TM, D_MODEL), F32),
        pltpu.VMEM((TM, POOL_WIDTH), BF16),
        pltpu.VMEM((TM, ATTN_WIDTH), BF16),
        pltpu.VMEM((N_Q_HEADS * WINDOW, 2 * WINDOW), F32),
    ]
    return pl.pallas_call(
        _prompt_kernel,
        grid=(batch, seq // TM),
        in_specs=[_SMEM, _SMEM, tile_block] + _weight_specs(),
        out_specs=(
            tile_block,
            seq_block((KV_WIDTH, WINDOW)),
            seq_block((KV_WIDTH, WINDOW)),
            pl.BlockSpec((POOL_STATE_ROWS, batch, POOL_WIDTH), lambda b, j: (0, 0, 0)),
        ),
        out_shape=out_shape,
        scratch_shapes=scratch,
        compiler_params=pltpu.CompilerParams(
            dimension_semantics=("arbitrary", "arbitrary"), vmem_limit_bytes=VMEM_LIMIT_BYTES),
        name="prompt_layer",
    )(relb, sinks, x, *weights)


def _sample_kernel(relb_ref, sinks_ref, x_ref, kt_ref, vt_ref, st_ref, gnorm_ref, pscale_ref, gfinal_ref,
                   w_in_ref, w_grp_ref, w_brp_ref, w_bra_ref, w_out_ref,
                   y_ref, knew_ref, vnew_ref, pnew_ref, o_in_ref, o_grp_ref, o_brp_ref, o_bra_ref, o_out_ref,
                   xn_s, h_s, u_s, zp_s, qh_s, k_s, v_s, za_s, gp_s, ga_s, pooled_s, att_s,
                   q2_s, kn2_s, vn2_s, o2_s, bias_s, wgrp_s, wbrp_s, wbra_s, wout_s):
    s = pl.program_id(0)
    n_steps = pl.num_programs(0)
    n_seq = x_ref.shape[0]
    rows_in = D_MODEL // W_STEPS

    @pl.when(s == 0)
    def _normalise():
        xn = _rmsnorm(x_ref[:, 0, :], gnorm_ref[...]).astype(BF16)
        for c in range(W_STEPS):
            xn_s[c] = xn[:, c * rows_in:(c + 1) * rows_in]
        h_s[...] = jnp.zeros((n_seq, IN_COLS), F32)

    @pl.when(s < W_STEPS)
    def _weight_chunk():
        w_in_bf = w_in_ref[0].astype(BF16)
        o_in_ref[...] = w_in_bf
        h_s[...] += _dot(xn_s[s], w_in_bf)
        for w_ref, o_ref, copy_s, axis in ((w_grp_ref, o_grp_ref, wgrp_s, 1), (w_brp_ref, o_brp_ref, wbrp_s, 0),
                                          (w_bra_ref, o_bra_ref, wbra_s, 0), (w_out_ref, o_out_ref, wout_s, 0)):
            w_bf = w_ref[0].astype(BF16)
            o_ref[...] = w_bf
            rows = w_bf.shape[axis]
            chunk_rows = pl.ds(pl.multiple_of(s * rows, rows), rows)
            if axis == 0:
                copy_s[chunk_rows, :] = w_bf
            else:
                copy_s[:, chunk_rows, :] = w_bf

    @pl.when(s == W_STEPS - 1)
    def _activations():
        slot = lax.broadcasted_iota(jnp.int32, (SUBLANES, WINDOW), 1)
        head = lax.broadcasted_iota(jnp.int32, (SUBLANES, WINDOW), 0)
        tbl = jnp.zeros((SUBLANES, WINDOW), F32)
        for h in range(N_Q_HEADS):
            tbl = jnp.where(head == h, _bias_table(WINDOW - slot, relb_ref, h), tbl)
        bias_s[...] = tbl

        u_s[...] = h_s[:, OFF_U:OFF_U + POOL_WIDTH]
        zp_s[...] = _silu(h_s[:, OFF_ZP:OFF_ZP + POOL_WIDTH])
        q_heads = _head_blocks(h_s[:, OFF_Q:OFF_Q + ATTN_WIDTH] * (HEAD_DIM ** -0.5))
        for h in range(N_Q_HEADS):
            qh_s[:, h * LANES:(h + 1) * LANES] = q_heads[h]
        k_s[...] = h_s[:, OFF_K:OFF_K + KV_WIDTH]
        v_s[...] = h_s[:, OFF_V:OFF_V + KV_WIDTH]
        za_s[...] = _silu(h_s[:, OFF_ZA:OFF_ZA + ATTN_WIDTH])
        gp_s[...] = _sigmoid(h_s[:, OFF_GP:OFF_GP + D_MODEL])
        ga_s[...] = _sigmoid(h_s[:, OFF_GA:OFF_GA + D_MODEL])

    @pl.when(s >= W_STEPS)
    def _sequence_chunk():
        i = s - W_STEPS
        chunk = pl.ds(pl.multiple_of(i * BC, BC), BC)

        u_new = u_s[chunk, :]
        n_hist = POOL_STATE_ROWS
        parts = []
        for g, w in enumerate(POOL_WINDOWS):
            cols = slice(g * POOL_GROUP_DIM, (g + 1) * POOL_GROUP_DIM)
            acc = u_new[:, cols]
            for r in range(n_hist - (w - 1), n_hist):
                acc = acc + st_ref[r, :, cols]
            parts.append(acc * (1.0 / w) - u_new[:, cols])
        pooled_s[chunk, :] = jnp.concatenate(parts, axis=-1).astype(BF16)
        pnew_ref[0:n_hist - 1] = st_ref[1:n_hist]
        pnew_ref[n_hist - 1] = u_new

        k_new = k_s[chunk, :]
        v_new = v_s[chunk, :]
        last_slot = lax.broadcasted_iota(jnp.int32, (KV_WIDTH, WINDOW), 1) == WINDOW - 1
        k_new_t = k_new.T
        v_new_t = v_new.T
        for b in range(BC):
            knew_ref[b] = jnp.where(last_slot, k_new_t[:, b:b + 1], pltpu.roll(kt_ref[b], WINDOW - 1, 1))
            vnew_ref[b] = jnp.where(last_slot, v_new_t[:, b:b + 1], pltpu.roll(vt_ref[b], WINDOW - 1, 1))

        for h in range(N_Q_HEADS):
            q2_s[pl.ds(h, BC, stride=N_Q_HEADS), :] = qh_s[chunk, h * LANES:(h + 1) * LANES]
            kn2_s[pl.ds(h, BC, stride=N_Q_HEADS), :] = k_new
            vn2_s[pl.ds(h, BC, stride=N_Q_HEADS), :] = v_new
        q3 = q2_s[...].reshape(BC, N_Q_HEADS, LANES)
        kn3 = kn2_s[...].reshape(BC, N_Q_HEADS, LANES)
        vn3 = vn2_s[...].reshape(BC, N_Q_HEADS, LANES)
        s3 = lax.dot_general(q3.astype(BF16), kt_ref[...].astype(BF16), (((2,), (1,)), ((0,), (0,))),
                             preferred_element_type=F32) + bias_s[...][None]
        s_self = jnp.sum(q3 * kn3, axis=-1, keepdims=True) + _per_head_column(lambda h: relb_ref[h, 0])[None]
        sink = _per_head_column(lambda h: sinks_ref[h])[None]
        m = jnp.maximum(jnp.maximum(jnp.max(s3, axis=-1, keepdims=True), s_self), sink)
        p3 = jnp.exp(s3 - m)
        p_self = jnp.exp(s_self - m)
        denom = jnp.sum(p3, axis=-1, keepdims=True) + p_self + jnp.exp(sink - m)
        o3 = lax.dot_general(p3.astype(BF16), vt_ref[...].astype(BF16), (((2,), (2,)), ((0,), (0,))),
                             preferred_element_type=F32)
        o3 = (o3 + p_self * vn3) * (1.0 / denom)
        o2_s[...] = o3.reshape(BC * N_Q_HEADS, LANES)
        outs = [o2_s[pl.ds(h, BC, stride=N_Q_HEADS), :] for h in range(N_Q_HEADS)]
        att_s[chunk, :] = (_gather_heads(outs) * za_s[chunk, :]).astype(BF16)

    @pl.when(s == n_steps - 1)
    def _merge_all():
        out = _residual_sum(x_ref[:, 0, :], pooled_s[...], zp_s[...], att_s[...], gp_s[...], ga_s[...],
                            wgrp_s, pscale_ref[...], wbrp_s, wbra_s, wout_s)
        y_ref[:, 0, :] = _rmsnorm(out, gfinal_ref[...])


def _sample_call(x, cache_kt, cache_vt, state_t, relb, sinks, g_norm, pool_scale, g_final,
                 w_in, w_grp, w_brp, w_bra, w_out):
    n_seq = x.shape[0]
    assert n_seq % BC == 0

    def chunk_rows(n):
        assert n % (W_STEPS * 2 * SUBLANES) == 0
        return n // W_STEPS

    r_in, r_grp, r_br, r_out = (chunk_rows(D_MODEL), chunk_rows(POOL_GROUP_DIM), chunk_rows(POOL_WIDTH),
                                chunk_rows(D_MODEL))

    def w_step(s):
        return jnp.minimum(s, W_STEPS - 1)

    def seq_step(s):
        return jnp.maximum(s - W_STEPS, 0)

    weight_in_specs = [
        pl.BlockSpec((1, r_in, IN_COLS), lambda s: (0, w_step(s), 0)),
        pl.BlockSpec((1, POOL_GROUPS, r_grp, POOL_GROUP_DIM), lambda s: (0, 0, w_step(s), 0)),
        pl.BlockSpec((1, r_br, D_MODEL), lambda s: (0, w_step(s), 0)),
        pl.BlockSpec((1, r_br, D_MODEL), lambda s: (0, w_step(s), 0)),
        pl.BlockSpec((1, r_out, D_MODEL), lambda s: (0, w_step(s), 0)),
    ]
    weight_out_specs = (
        pl.BlockSpec((r_in, IN_COLS), lambda s: (w_step(s), 0)),
        pl.BlockSpec((POOL_GROUPS, r_grp, POOL_GROUP_DIM), lambda s: (0, w_step(s), 0)),
        pl.BlockSpec((r_br, D_MODEL), lambda s: (w_step(s), 0)),
        pl.BlockSpec((r_br, D_MODEL), lambda s: (w_step(s), 0)),
        pl.BlockSpec((r_out, D_MODEL), lambda s: (w_step(s), 0)),
    )
    out_shape = (
        jax.ShapeDtypeStruct((n_seq, 1, D_MODEL), F32),
        jax.ShapeDtypeStruct((n_seq, KV_WIDTH, WINDOW), F32),
        jax.ShapeDtypeStruct((n_seq, KV_WIDTH, WINDOW), F32),
        jax.ShapeDtypeStruct((POOL_STATE_ROWS, n_seq, POOL_WIDTH), F32),
        jax.ShapeDtypeStruct((D_MODEL, IN_COLS), BF16),
        jax.ShapeDtypeStruct((POOL_GROUPS, POOL_GROUP_DIM, POOL_GROUP_DIM), BF16),
        jax.ShapeDtypeStruct((POOL_WIDTH, D_MODEL), BF16),
        jax.ShapeDtypeStruct((ATTN_WIDTH, D_MODEL), BF16),
        jax.ShapeDtypeStruct((D_MODEL, D_MODEL), BF16),
    )
    scratch = [
        pltpu.VMEM((W_STEPS, n_seq, D_MODEL // W_STEPS), BF16),
        pltpu.VMEM((n_seq, IN_COLS), F32),
        pltpu.VMEM((n_seq, POOL_WIDTH), F32),
        pltpu.VMEM((n_seq, POOL_WIDTH), F32),
        pltpu.VMEM((n_seq, N_Q_HEADS * LANES), F32),
        pltpu.VMEM((n_seq, KV_WIDTH), F32),
        pltpu.VMEM((n_seq, KV_WIDTH), F32),
        pltpu.VMEM((n_seq, ATTN_WIDTH), F32),
        pltpu.VMEM((n_seq, D_MODEL), F32),
        pltpu.VMEM((n_seq, D_MODEL), F32),
        pltpu.VMEM((n_seq, POOL_WIDTH), BF16),
        pltpu.VMEM((n_seq, ATTN_WIDTH), BF16),
        pltpu.VMEM((BC * N_Q_HEADS, LANES), F32),
        pltpu.VMEM((BC * N_Q_HEADS, LANES), F32),
        pltpu.VMEM((BC * N_Q_HEADS, LANES), F32),
        pltpu.VMEM((BC * N_Q_HEADS, LANES), F32),
        pltpu.VMEM((SUBLANES, WINDOW), F32),
        pltpu.VMEM((POOL_GROUPS, POOL_GROUP_DIM, POOL_GROUP_DIM), BF16),
        pltpu.VMEM((POOL_WIDTH, D_MODEL), BF16),
        pltpu.VMEM((ATTN_WIDTH, D_MODEL), BF16),
        pltpu.VMEM((D_MODEL, D_MODEL), BF16),
    ]
    cache = pl.BlockSpec((BC, KV_WIDTH, WINDOW), lambda s: (seq_step(s), 0, 0))
    hist = pl.BlockSpec((POOL_STATE_ROWS, BC, POOL_WIDTH), lambda s: (0, seq_step(s), 0))
    return pl.pallas_call(
        _sample_kernel,
        grid=(W_STEPS + n_seq // BC,),
        in_specs=[_SMEM, _SMEM, _resident((n_seq, 1, D_MODEL)), cache, cache, hist,
                  _resident((1, D_MODEL)), _resident((1, POOL_WIDTH)), _resident((1, D_MODEL))] + weight_in_specs,
        out_specs=(pl.BlockSpec((n_seq, 1, D_MODEL), lambda s: (0, 0, 0)), cache, cache, hist) + weight_out_specs,
        out_shape=out_shape,
        scratch_shapes=scratch,
        compiler_params=pltpu.CompilerParams(
            dimension_semantics=("arbitrary",), vmem_limit_bytes=VMEM_LIMIT_BYTES),
        name="sample_layer",
    )(relb, sinks, x, cache_kt, cache_vt, state_t, g_norm, pool_scale, g_final, w_in, w_grp, w_brp, w_bra, w_out)


def _cache_as_kd_slot(cache):
    n = cache.shape[0]
    return cache.transpose(0, 2, 3, 1).reshape(n, KV_WIDTH, WINDOW)


def _cache_from_kd_slot(cache_t):
    n = cache_t.shape[0]
    return cache_t.reshape(n, N_KV_HEADS, HEAD_DIM, WINDOW).transpose(0, 3, 1, 2)


def kernel(x_prompt, x_sample, cache_k, cache_v, state_pool, rel_bias, g_norm, w_in, pool_w_grp, pool_scale, attn_sinks, w_br_pool, w_br_attn, w_out, g_final):
    depth = g_norm.shape[0]
    assert depth == 1 and x_sample.shape[1] == 1
    l = 0
    sinks = attn_sinks[l]
    relb = rel_bias.T
    g_in, p_scale, g_out = g_norm[l].reshape(1, D_MODEL), pool_scale[l].reshape(1, POOL_WIDTH), g_final.reshape(1, D_MODEL)

    y_s, k_s, v_s, pool_s, w_in_bf, w_grp_bf, w_brp_bf, w_bra_bf, w_out_bf = _sample_call(
        x_sample, _cache_as_kd_slot(cache_k[l]), _cache_as_kd_slot(cache_v[l]),
        state_pool[l].transpose(1, 0, 2), relb, sinks, g_in, p_scale, g_out,
        w_in, pool_w_grp, w_br_pool, w_br_attn, w_out)
    weights = (g_in, w_in_bf, w_grp_bf, p_scale, w_brp_bf, w_bra_bf, w_out_bf, g_out)
    y_p, k_p, v_p, pool_p = _prompt_call(x_prompt, relb, sinks, weights)
    return (
        y_p,
        y_s,
        _cache_from_kd_slot(k_p)[None],
        _cache_from_kd_slot(v_p)[None],
        pool_p.transpose(1, 0, 2)[None],
        _cache_from_kd_slot(k_s)[None],
        _cache_from_kd_slot(v_s)[None],
        pool_s.transpose(1, 0, 2)[None],
    )
```

```python
import math

import jax
import jax.numpy as jnp
from jax import lax
from jax.experimental import pallas as pl
from jax.experimental.pallas import tpu as pltpu

D_MODEL = 1024
POOL_WINDOWS = (2, 4, 8, 16)
POOL_GROUPS = len(POOL_WINDOWS)
POOL_WIDTH = D_MODEL // 2
POOL_GROUP_DIM = POOL_WIDTH // POOL_GROUPS
POOL_STATE_ROWS = max(POOL_WINDOWS) - 1
HEAD_DIM = 64
N_KV_HEADS = 2
ATTN_WIDTH = D_MODEL // 2
N_Q_HEADS = ATTN_WIDTH // HEAD_DIM
GQA_GROUP = N_Q_HEADS // N_KV_HEADS
KV_WIDTH = N_KV_HEADS * HEAD_DIM
WINDOW = 128
N_BUCKETS = 32
MAX_DISTANCE = 128
RMS_EPS = 1e-6
IN_COLS = 2 * POOL_WIDTH + 2 * ATTN_WIDTH + 2 * KV_WIDTH + 2 * D_MODEL

OFF_U = 0
OFF_ZP = OFF_U + POOL_WIDTH
OFF_Q = OFF_ZP + POOL_WIDTH
OFF_K = OFF_Q + ATTN_WIDTH
OFF_V = OFF_K + KV_WIDTH
OFF_ZA = OFF_V + KV_WIDTH
OFF_GP = OFF_ZA + ATTN_WIDTH
OFF_GA = OFF_GP + D_MODEL

LANES = 128
SUBLANES = 8
POOL_HDR = 16
TM = 1024
BC = 32
W_STEPS = 4
OUT_CHUNKS = 4
VMEM_LIMIT_BYTES = 56 * 1024 * 1024

assert KV_WIDTH == LANES and 2 * HEAD_DIM == LANES and GQA_GROUP % 2 == 0

F32 = jnp.float32
BF16 = jnp.bfloat16
NEG_INF = float("-inf")


def _sigmoid(x):
    return 1.0 / (1.0 + jnp.exp(-x))


def _silu(x):
    return x * _sigmoid(x)


def _rmsnorm(x, g):
    return x * lax.rsqrt(jnp.mean(x * x, axis=-1, keepdims=True) + RMS_EPS) * g


def _dot(a, b):
    return jnp.dot(a, b, preferred_element_type=F32)


def _t5_bucket(rel):
    n = jnp.maximum(rel, 0)
    max_exact = N_BUCKETS // 2
    nf = jnp.maximum(n, 1).astype(F32)
    v = jnp.log(nf / max_exact) / math.log(MAX_DISTANCE / max_exact) * (N_BUCKETS - max_exact)
    steps = jnp.zeros(rel.shape, jnp.int32)
    for k in range(1, N_BUCKETS - max_exact):
        steps = steps + jnp.where(v >= k, 1, 0)
    return jnp.where(n < max_exact, n, max_exact + steps)


def _bias_table(rel, relb_ref, h):
    bucket = _t5_bucket(rel)

    def body(b, tbl):
        return jnp.where(bucket == b, relb_ref[h, b], tbl)

    tbl = lax.fori_loop(0, N_BUCKETS, body, jnp.zeros(rel.shape, F32))
    return jnp.where((rel >= 0) & (rel < WINDOW), tbl, NEG_INF)


def _per_head_column(ref_scalar):
    head = lax.broadcasted_iota(jnp.int32, (N_Q_HEADS, 1), 0)
    col = jnp.zeros((N_Q_HEADS, 1), F32)
    for h in range(N_Q_HEADS):
        col = jnp.where(head == h, ref_scalar(h), col)
    return col


def _project(xn_ref, w_in_ref, off, width):
    return _dot(xn_ref[...], w_in_ref[:, off:off + width])


def _kv_lane_mask(shape):
    return lax.broadcasted_iota(jnp.int32, shape, len(shape) - 1) < HEAD_DIM


def _head_pairs():
    for grp in range(ATTN_WIDTH // LANES):
        kv, pair = divmod(grp, GQA_GROUP // 2)
        yield grp, kv, kv * GQA_GROUP + 2 * pair, kv * GQA_GROUP + 2 * pair + 1


def _head_blocks(q):
    low = _kv_lane_mask((q.shape[0], LANES))
    blocks = [None] * N_Q_HEADS
    for grp, kv, even, odd in _head_pairs():
        a = q[:, grp * LANES:(grp + 1) * LANES]
        swapped = pltpu.roll(a, HEAD_DIM, 1)
        if kv == 0:
            blocks[even], blocks[odd] = jnp.where(low, a, 0.0), jnp.where(low, swapped, 0.0)
        else:
            blocks[even], blocks[odd] = jnp.where(low, 0.0, swapped), jnp.where(low, 0.0, a)
    return blocks


def _gather_heads(outs):
    low = _kv_lane_mask(outs[0].shape)
    groups = []
    for _, kv, even, odd in _head_pairs():
        if kv == 0:
            groups.append(jnp.where(low, outs[even], pltpu.roll(outs[odd], HEAD_DIM, 1)))
        else:
            groups.append(jnp.where(low, pltpu.roll(outs[even], HEAD_DIM, 1), outs[odd]))
    return jnp.concatenate(groups, axis=-1)


def _pool_windows(x_ext, inv_cnt):
    outs = []
    for g, w in enumerate(POOL_WINDOWS):
        xg = x_ext[:, g * POOL_GROUP_DIM:(g + 1) * POOL_GROUP_DIM]
        acc = xg
        shift = 1
        while shift < w:
            acc = acc + pltpu.roll(acc, shift, 0)
            shift *= 2
        outs.append(acc[POOL_HDR:] * inv_cnt[g] - xg[POOL_HDR:])
    return jnp.concatenate(outs, axis=-1)


def _merged_branches(pooled, zp_act, att_act, sig_gp, sig_ga, w_grp_ref, pool_scale, w_brp_ref, w_bra_ref):
    pg = jnp.concatenate(
        [_dot(pooled[:, g * POOL_GROUP_DIM:(g + 1) * POOL_GROUP_DIM], w_grp_ref[g])
         for g in range(POOL_GROUPS)], axis=-1)
    br_pool = _dot((pg * pool_scale * zp_act).astype(BF16), w_brp_ref[...])
    br_attn = _dot(att_act, w_bra_ref[...])
    return (sig_gp * br_pool + sig_ga * br_attn).astype(BF16)


def _prompt_kernel(relb_ref, sinks_ref, x_ref, gnorm_ref, w_in_ref, w_grp_ref, pscale_ref,
                   w_brp_ref, w_bra_ref, w_out_ref, gfinal_ref,
                   y_ref, knew_ref, vnew_ref, pnew_ref,
                   xn_s, uext_s, zp_s, qh_s, kband_s, vband_s, za_s, gp_s, ga_s, pooled_s, att_s, bias_s):
    b = pl.program_id(0)
    j = pl.program_id(1)
    nblk = TM // WINDOW
    hrows = N_Q_HEADS * WINDOW

    @pl.when((b == 0) & (j == 0))
    def _build_bias():
        qi = lax.broadcasted_iota(jnp.int32, (WINDOW, 2 * WINDOW), 0)
        kc = lax.broadcasted_iota(jnp.int32, (WINDOW, 2 * WINDOW), 1)
        for h in range(N_Q_HEADS):
            bias_s[h * WINDOW:(h + 1) * WINDOW, :] = _bias_table(qi - kc + WINDOW, relb_ref, h)

    @pl.when(j == 0)
    def _reset_carry():
        uext_s[0:POOL_HDR, :] = jnp.zeros((POOL_HDR, POOL_WIDTH), F32)
        kband_s[0:WINDOW, :] = jnp.zeros((WINDOW, KV_WIDTH), BF16)
        vband_s[0:WINDOW, :] = jnp.zeros((WINDOW, KV_WIDTH), BF16)

    def _tile():
        chunk = TM // OUT_CHUNKS
        for ch in range(OUT_CHUNKS):
            r = slice(ch * chunk, (ch + 1) * chunk)
            xn_s[r, :] = _rmsnorm(x_ref[0, r, :], gnorm_ref[...]).astype(BF16)
            uext_s[POOL_HDR + ch * chunk:POOL_HDR + (ch + 1) * chunk, :] = _dot(
                xn_s[r, :], w_in_ref[:, OFF_U:OFF_U + POOL_WIDTH])
        q_heads = _head_blocks(_project(xn_s, w_in_ref, OFF_Q, ATTN_WIDTH) * (HEAD_DIM ** -0.5))
        for c in range(nblk):
            for h in range(N_Q_HEADS):
                r0 = c * hrows + h * WINDOW
                qh_s[r0:r0 + WINDOW, :] = q_heads[h][c * WINDOW:(c + 1) * WINDOW].astype(BF16)

        pos = j * TM + lax.broadcasted_iota(jnp.int32, (TM, 1), 0)
        inv_cnt = [1.0 / jnp.minimum(pos + 1, w).astype(F32) for w in POOL_WINDOWS]
        x_ext = uext_s[...]
        pooled_s[...] = _pool_windows(x_ext, inv_cnt).astype(BF16)
        pnew_ref[:, pl.ds(b, 1), :] = x_ext[POOL_HDR + TM - POOL_STATE_ROWS:, :][:, None, :]
        uext_s[0:POOL_HDR, :] = x_ext[TM:, :]

        kv = _project(xn_s, w_in_ref, OFF_K, 2 * KV_WIDTH)
        kband_s[WINDOW:, :] = kv[:, :KV_WIDTH].astype(BF16)
        vband_s[WINDOW:, :] = kv[:, KV_WIDTH:].astype(BF16)
        knew_ref[0] = kv[TM - WINDOW:, :KV_WIDTH].T
        vnew_ref[0] = kv[TM - WINDOW:, KV_WIDTH:].T
        za_s[...] = _silu(_project(xn_s, w_in_ref, OFF_ZA, ATTN_WIDTH))

        first_neg = jnp.where(j == 0, NEG_INF, 0.0).astype(F32)
        for c in range(nblk):
            rows = slice(c * WINDOW, (c + 1) * WINDOW)
            k_c = kband_s[c * WINDOW:(c + 2) * WINDOW, :]
            v_c = vband_s[c * WINDOW:(c + 2) * WINDOW, :]
            s_all = lax.dot_general(qh_s[c * hrows:(c + 1) * hrows, :], k_c, (((1,), (1,)), ((), ())),
                                    preferred_element_type=F32)
            p_parts, inv_parts = [], []
            for h in range(N_Q_HEADS):
                head_rows = slice(h * WINDOW, (h + 1) * WINDOW)
                sc = s_all[head_rows, :] + bias_s[head_rows, :]
                if c == 0:
                    sc = jnp.concatenate([sc[:, :WINDOW] + first_neg, sc[:, WINDOW:]], axis=-1)
                sink = sinks_ref[h]
                m = jnp.maximum(jnp.max(sc, axis=-1, keepdims=True), sink)
                p = jnp.exp(sc - m)
                denom = jnp.sum(p, axis=-1, keepdims=True) + jnp.exp(sink - m)
                p_parts.append(p.astype(BF16))
                inv_parts.append(1.0 / denom)
            o_all = _dot(jnp.concatenate(p_parts, axis=0), v_c)
            outs = [o_all[h * WINDOW:(h + 1) * WINDOW, :] * inv_parts[h] for h in range(N_Q_HEADS)]
            att_s[rows, :] = (_gather_heads(outs) * za_s[rows, :]).astype(BF16)
        kband_s[0:WINDOW, :] = kband_s[TM:, :]
        vband_s[0:WINDOW, :] = vband_s[TM:, :]

        zp_s[...] = _silu(_project(xn_s, w_in_ref, OFF_ZP, POOL_WIDTH))
        for half in range(2):
            cols = slice(half * POOL_WIDTH, (half + 1) * POOL_WIDTH)
            gp_s[:, cols] = _sigmoid(_project(xn_s, w_in_ref, OFF_GP + half * POOL_WIDTH, POOL_WIDTH))
            ga_s[:, cols] = _sigmoid(_project(xn_s, w_in_ref, OFF_GA + half * POOL_WIDTH, POOL_WIDTH))

        merged = _merged_branches(pooled_s[...], zp_s[...], att_s[...], gp_s[...], ga_s[...],
                                  w_grp_ref, pscale_ref[...], w_brp_ref, w_bra_ref)
        for ch in range(OUT_CHUNKS):
            r = slice(ch * chunk, (ch + 1) * chunk)
            out = x_ref[0, r, :] + _dot(merged[r, :], w_out_ref[...])
            y_ref[0, r, :] = _rmsnorm(out, gfinal_ref[...])

    _tile()


def _resident(shape):
    return pl.BlockSpec(shape, lambda *_: (0,) * len(shape), pipeline_mode=pl.Buffered(1))


_SMEM = pl.BlockSpec(memory_space=pltpu.SMEM)


def _weight_specs():
    return [
        _resident((1, D_MODEL)),
        _resident((D_MODEL, IN_COLS)),
        _resident((POOL_GROUPS, POOL_GROUP_DIM, POOL_GROUP_DIM)),
        _resident((1, POOL_WIDTH)),
        _resident((POOL_WIDTH, D_MODEL)),
        _resident((ATTN_WIDTH, D_MODEL)),
        _resident((D_MODEL, D_MODEL)),
        _resident((1, D_MODEL)),
    ]


def _prompt_call(x, relb, sinks, weights):
    batch, seq, _ = x.shape
    assert seq % TM == 0 and TM % WINDOW == 0 and TM >= POOL_HDR
    tile_block = pl.BlockSpec((1, TM, D_MODEL), lambda b, j: (b, j, 0))

    def seq_block(shape):
        return pl.BlockSpec((1,) + shape, lambda b, j: (b, 0, 0))

    out_shape = (
        jax.ShapeDtypeStruct((batch, seq, D_MODEL), F32),
        jax.ShapeDtypeStruct((batch, KV_WIDTH, WINDOW), F32),
        jax.ShapeDtypeStruct((batch, KV_WIDTH, WINDOW), F32),
        jax.ShapeDtypeStruct((POOL_STATE_ROWS, batch, POOL_WIDTH), F32),
    )
    scratch = [
        pltpu.VMEM((TM, D_MODEL), BF16),
        pltpu.VMEM((POOL_HDR + TM, POOL_WIDTH), F32),
        pltpu.VMEM((TM, POOL_WIDTH), F32),
        pltpu.VMEM((TM * N_Q_HEADS, LANES), BF16),
        pltpu.VMEM((WINDOW + TM, KV_WIDTH), BF16),
        pltpu.VMEM((WINDOW + TM, KV_WIDTH), BF16),
        pltpu.VMEM((TM, ATTN_WIDTH), F32),
        pltpu.VMEM((TM, D_MODEL), F32),
        pltpu.VMEM((TM, D_MODEL), F32),
        pltpu.VMEM((TM, POOL_WIDTH), BF16),
        pltpu.VMEM((TM, ATTN_WIDTH), BF16),
        pltpu.VMEM((N_Q_HEADS * WINDOW, 2 * WINDOW), F32),
    ]
    return pl.pallas_call(
        _prompt_kernel,
        grid=(batch, seq // TM),
        in_specs=[_SMEM, _SMEM, tile_block] + _weight_specs(),
        out_specs=(
            tile_block,
            seq_block((KV_WIDTH, WINDOW)),
            seq_block((KV_WIDTH, WINDOW)),
            pl.BlockSpec((POOL_STATE_ROWS, batch, POOL_WIDTH), lambda b, j: (0, 0, 0)),
        ),
        out_shape=out_shape,
        scratch_shapes=scratch,
        compiler_params=pltpu.CompilerParams(
            dimension_semantics=("arbitrary", "arbitrary"), vmem_limit_bytes=VMEM_LIMIT_BYTES),
        name="prompt_layer",
    )(relb, sinks, x, *weights)


def _last_slot_columns(rows, place_ref):
    t = rows.T
    hi = t.astype(BF16)
    rest = t - hi.astype(F32)
    mid = rest.astype(BF16)
    lo = (rest - mid.astype(F32)).astype(BF16)
    return _dot(jnp.concatenate([hi, mid, lo], axis=1), place_ref[...])


def _placement_matrix():
    r = lax.broadcasted_iota(jnp.int32, (3 * BC, BC * WINDOW), 0)
    c = lax.broadcasted_iota(jnp.int32, (3 * BC, BC * WINDOW), 1)
    hit = (c == (r % BC) * WINDOW + WINDOW - 1)
    return jnp.where(hit, 1.0, 0.0).astype(BF16)


def _sample_kernel(relb_ref, sinks_ref, x_ref, kt_ref, vt_ref, st_ref, gnorm_ref, pscale_ref, gfinal_ref,
                   w_in_ref, w_grp_ref, w_brp_ref, w_bra_ref, w_out_ref,
                   y_ref, knew_ref, vnew_ref, pnew_ref, o_in_ref, o_grp_ref, o_brp_ref, o_bra_ref, o_out_ref,
                   xn_s, h_s, u_s, zp_s, qh_s, k_s, v_s, za_s, gp_s, ga_s, pooled_s, att_s,
                   q2_s, kn2_s, vn2_s, o2_s, bias_s, place_s, wgrp_s, wbrp_s, wbra_s, wout_s):
    s = pl.program_id(0)
    n_steps = pl.num_programs(0)
    n_seq = x_ref.shape[0]
    rows_in = D_MODEL // W_STEPS

    @pl.when(s == 0)
    def _normalise():
        xn = _rmsnorm(x_ref[:, 0, :], gnorm_ref[...]).astype(BF16)
        for c in range(W_STEPS):
            xn_s[c] = xn[:, c * rows_in:(c + 1) * rows_in]
        h_s[...] = jnp.zeros((n_seq, IN_COLS), F32)

    @pl.when(s < W_STEPS)
    def _weight_chunk():
        w_in_bf = w_in_ref[0].astype(BF16)
        o_in_ref[...] = w_in_bf
        h_s[...] += _dot(xn_s[s], w_in_bf)
        for w_ref, o_ref, copy_s, axis in ((w_grp_ref, o_grp_ref, wgrp_s, 1), (w_brp_ref, o_brp_ref, wbrp_s, 0),
                                          (w_bra_ref, o_bra_ref, wbra_s, 0), (w_out_ref, o_out_ref, wout_s, 0)):
            w_bf = w_ref[0].astype(BF16)
            o_ref[...] = w_bf
            rows = w_bf.shape[axis]
            chunk_rows = pl.ds(pl.multiple_of(s * rows, rows), rows)
            if axis == 0:
                copy_s[chunk_rows, :] = w_bf
            else:
                copy_s[:, chunk_rows, :] = w_bf

    @pl.when(s == W_STEPS - 1)
    def _activations():
        slot = lax.broadcasted_iota(jnp.int32, (SUBLANES, WINDOW), 1)
        head = lax.broadcasted_iota(jnp.int32, (SUBLANES, WINDOW), 0)
        tbl = jnp.zeros((SUBLANES, WINDOW), F32)
        for h in range(N_Q_HEADS):
            tbl = jnp.where(head == h, _bias_table(WINDOW - slot, relb_ref, h), tbl)
        bias_s[...] = tbl
        place_s[...] = _placement_matrix()

        u_s[...] = h_s[:, OFF_U:OFF_U + POOL_WIDTH]
        zp_s[...] = _silu(h_s[:, OFF_ZP:OFF_ZP + POOL_WIDTH])
        q_heads = _head_blocks(h_s[:, OFF_Q:OFF_Q + ATTN_WIDTH] * (HEAD_DIM ** -0.5))
        for h in range(N_Q_HEADS):
            qh_s[:, h * LANES:(h + 1) * LANES] = q_heads[h]
        k_s[...] = h_s[:, OFF_K:OFF_K + KV_WIDTH]
        v_s[...] = h_s[:, OFF_V:OFF_V + KV_WIDTH]
        za_s[...] = _silu(h_s[:, OFF_ZA:OFF_ZA + ATTN_WIDTH])
        gp_s[...] = _sigmoid(h_s[:, OFF_GP:OFF_GP + D_MODEL])
        ga_s[...] = _sigmoid(h_s[:, OFF_GA:OFF_GA + D_MODEL])

    @pl.when(s >= W_STEPS)
    def _sequence_chunk():
        i = s - W_STEPS
        chunk = pl.ds(pl.multiple_of(i * BC, BC), BC)

        u_new = u_s[chunk, :]
        n_hist = POOL_STATE_ROWS
        parts = []
        for g, w in enumerate(POOL_WINDOWS):
            cols = slice(g * POOL_GROUP_DIM, (g + 1) * POOL_GROUP_DIM)
            acc = u_new[:, cols]
            for r in range(n_hist - (w - 1), n_hist):
                acc = acc + st_ref[r, :, cols]
            parts.append(acc * (1.0 / w) - u_new[:, cols])
        pooled_s[chunk, :] = jnp.concatenate(parts, axis=-1).astype(BF16)
        pnew_ref[0:n_hist - 1] = st_ref[1:n_hist]
        pnew_ref[n_hist - 1] = u_new

        k_new = k_s[chunk, :]
        v_new = v_s[chunk, :]
        last_slot = lax.broadcasted_iota(jnp.int32, (KV_WIDTH, WINDOW), 1) == WINDOW - 1
        k_cols = _last_slot_columns(k_new, place_s)
        v_cols = _last_slot_columns(v_new, place_s)
        for b in range(BC):
            lanes = slice(b * WINDOW, (b + 1) * WINDOW)
            knew_ref[b] = jnp.where(last_slot, k_cols[:, lanes], pltpu.roll(kt_ref[b], WINDOW - 1, 1))
            vnew_ref[b] = jnp.where(last_slot, v_cols[:, lanes], pltpu.roll(vt_ref[b], WINDOW - 1, 1))

        for h in range(N_Q_HEADS):
            q2_s[pl.ds(h, BC, stride=N_Q_HEADS), :] = qh_s[chunk, h * LANES:(h + 1) * LANES]
            kn2_s[pl.ds(h, BC, stride=N_Q_HEADS), :] = k_new
            vn2_s[pl.ds(h, BC, stride=N_Q_HEADS), :] = v_new
        q3 = q2_s[...].reshape(BC, N_Q_HEADS, LANES)
        kn3 = kn2_s[...].reshape(BC, N_Q_HEADS, LANES)
        vn3 = vn2_s[...].reshape(BC, N_Q_HEADS, LANES)
        s3 = lax.dot_general(q3.astype(BF16), kt_ref[...].astype(BF16), (((2,), (1,)), ((0,), (0,))),
                             preferred_element_type=F32) + bias_s[...][None]
        s_self = jnp.sum(q3 * kn3, axis=-1, keepdims=True) + _per_head_column(lambda h: relb_ref[h, 0])[None]
        sink = _per_head_column(lambda h: sinks_ref[h])[None]
        m = jnp.maximum(jnp.maximum(jnp.max(s3, axis=-1, keepdims=True), s_self), sink)
        p3 = jnp.exp(s3 - m)
        p_self = jnp.exp(s_self - m)
        denom = jnp.sum(p3, axis=-1, keepdims=True) + p_self + jnp.exp(sink - m)
        o3 = lax.dot_general(p3.astype(BF16), vt_ref[...].astype(BF16), (((2,), (2,)), ((0,), (0,))),
                             preferred_element_type=F32)
        o3 = (o3 + p_self * vn3) * (1.0 / denom)
        o2_s[...] = o3.reshape(BC * N_Q_HEADS, LANES)
        outs = [o2_s[pl.ds(h, BC, stride=N_Q_HEADS), :] for h in range(N_Q_HEADS)]
        att_s[chunk, :] = (_gather_heads(outs) * za_s[chunk, :]).astype(BF16)

    @pl.when(s == n_steps - 1)
    def _merge_all():
        merged = _merged_branches(pooled_s[...], zp_s[...], att_s[...], gp_s[...], ga_s[...],
                                  wgrp_s, pscale_ref[...], wbrp_s, wbra_s)
        out = x_ref[:, 0, :] + _dot(merged, wout_s[...])
        y_ref[:, 0, :] = _rmsnorm(out, gfinal_ref[...])


def _sample_call(x, cache_kt, cache_vt, state_t, relb, sinks, g_norm, pool_scale, g_final,
                 w_in, w_grp, w_brp, w_bra, w_out):
    n_seq = x.shape[0]
    assert n_seq % BC == 0

    def chunk_rows(n):
        assert n % (W_STEPS * 2 * SUBLANES) == 0
        return n // W_STEPS

    r_in, r_grp, r_br, r_out = (chunk_rows(D_MODEL), chunk_rows(POOL_GROUP_DIM), chunk_rows(POOL_WIDTH),
                                chunk_rows(D_MODEL))

    def w_step(s):
        return jnp.minimum(s, W_STEPS - 1)

    def seq_step(s):
        return jnp.maximum(s - W_STEPS, 0)

    weight_in_specs = [
        pl.BlockSpec((1, r_in, IN_COLS), lambda s: (0, w_step(s), 0)),
        pl.BlockSpec((1, POOL_GROUPS, r_grp, POOL_GROUP_DIM), lambda s: (0, 0, w_step(s), 0)),
        pl.BlockSpec((1, r_br, D_MODEL), lambda s: (0, w_step(s), 0)),
        pl.BlockSpec((1, r_br, D_MODEL), lambda s: (0, w_step(s), 0)),
        pl.BlockSpec((1, r_out, D_MODEL), lambda s: (0, w_step(s), 0)),
    ]
    weight_out_specs = (
        pl.BlockSpec((r_in, IN_COLS), lambda s: (w_step(s), 0)),
        pl.BlockSpec((POOL_GROUPS, r_grp, POOL_GROUP_DIM), lambda s: (0, w_step(s), 0)),
        pl.BlockSpec((r_br, D_MODEL), lambda s: (w_step(s), 0)),
        pl.BlockSpec((r_br, D_MODEL), lambda s: (w_step(s), 0)),
        pl.BlockSpec((r_out, D_MODEL), lambda s: (w_step(s), 0)),
    )
    out_shape = (
        jax.ShapeDtypeStruct((n_seq, 1, D_MODEL), F32),
        jax.ShapeDtypeStruct((n_seq, KV_WIDTH, WINDOW), F32),
        jax.ShapeDtypeStruct((n_seq, KV_WIDTH, WINDOW), F32),
        jax.ShapeDtypeStruct((POOL_STATE_ROWS, n_seq, POOL_WIDTH), F32),
        jax.ShapeDtypeStruct((D_MODEL, IN_COLS), BF16),
        jax.ShapeDtypeStruct((POOL_GROUPS, POOL_GROUP_DIM, POOL_GROUP_DIM), BF16),
        jax.ShapeDtypeStruct((POOL_WIDTH, D_MODEL), BF16),
        jax.ShapeDtypeStruct((ATTN_WIDTH, D_MODEL), BF16),
        jax.ShapeDtypeStruct((D_MODEL, D_MODEL), BF16),
    )
    scratch = [
        pltpu.VMEM((W_STEPS, n_seq, D_MODEL // W_STEPS), BF16),
        pltpu.VMEM((n_seq, IN_COLS), F32),
        pltpu.VMEM((n_seq, POOL_WIDTH), F32),
        pltpu.VMEM((n_seq, POOL_WIDTH), F32),
        pltpu.VMEM((n_seq, N_Q_HEADS * LANES), F32),
        pltpu.VMEM((n_seq, KV_WIDTH), F32),
        pltpu.VMEM((n_seq, KV_WIDTH), F32),
        pltpu.VMEM((n_seq, ATTN_WIDTH), F32),
        pltpu.VMEM((n_seq, D_MODEL), F32),
        pltpu.VMEM((n_seq, D_MODEL), F32),
        pltpu.VMEM((n_seq, POOL_WIDTH), BF16),
        pltpu.VMEM((n_seq, ATTN_WIDTH), BF16),
        pltpu.VMEM((BC * N_Q_HEADS, LANES), F32),
        pltpu.VMEM((BC * N_Q_HEADS, LANES), F32),
        pltpu.VMEM((BC * N_Q_HEADS, LANES), F32),
        pltpu.VMEM((BC * N_Q_HEADS, LANES), F32),
        pltpu.VMEM((SUBLANES, WINDOW), F32),
        pltpu.VMEM((3 * BC, BC * WINDOW), BF16),
        pltpu.VMEM((POOL_GROUPS, POOL_GROUP_DIM, POOL_GROUP_DIM), BF16),
        pltpu.VMEM((POOL_WIDTH, D_MODEL), BF16),
        pltpu.VMEM((ATTN_WIDTH, D_MODEL), BF16),
        pltpu.VMEM((D_MODEL, D_MODEL), BF16),
    ]
    cache = pl.BlockSpec((BC, KV_WIDTH, WINDOW), lambda s: (seq_step(s), 0, 0))
    hist = pl.BlockSpec((POOL_STATE_ROWS, BC, POOL_WIDTH), lambda s: (0, seq_step(s), 0))
    return pl.pallas_call(
        _sample_kernel,
        grid=(W_STEPS + n_seq // BC,),
        in_specs=[_SMEM, _SMEM, _resident((n_seq, 1, D_MODEL)), cache, cache, hist,
                  _resident((1, D_MODEL)), _resident((1, POOL_WIDTH)), _resident((1, D_MODEL))] + weight_in_specs,
        out_specs=(pl.BlockSpec((n_seq, 1, D_MODEL), lambda s: (0, 0, 0)), cache, cache, hist) + weight_out_specs,
        out_shape=out_shape,
        scratch_shapes=scratch,
        compiler_params=pltpu.CompilerParams(
            dimension_semantics=("arbitrary",), vmem_limit_bytes=VMEM_LIMIT_BYTES),
        name="sample_layer",
    )(relb, sinks, x, cache_kt, cache_vt, state_t, g_norm, pool_scale, g_final, w_in, w_grp, w_brp, w_bra, w_out)


def _cache_as_kd_slot(cache):
    n = cache.shape[0]
    return cache.transpose(0, 2, 3, 1).reshape(n, KV_WIDTH, WINDOW)


def _cache_from_kd_slot(cache_t):
    n = cache_t.shape[0]
    return cache_t.reshape(n, N_KV_HEADS, HEAD_DIM, WINDOW).transpose(0, 3, 1, 2)


def kernel(x_prompt, x_sample, cache_k, cache_v, state_pool, rel_bias, g_norm, w_in, pool_w_grp, pool_scale, attn_sinks, w_br_pool, w_br_attn, w_out, g_final):
    depth = g_norm.shape[0]
    assert depth == 1 and x_sample.shape[1] == 1
    l = 0
    sinks = attn_sinks[l]
    relb = rel_bias.T
    g_in, p_scale, g_out = g_norm[l].reshape(1, D_MODEL), pool_scale[l].reshape(1, POOL_WIDTH), g_final.reshape(1, D_MODEL)

    y_s, k_s, v_s, pool_s, w_in_bf, w_grp_bf, w_brp_bf, w_bra_bf, w_out_bf = _sample_call(
        x_sample, _cache_as_kd_slot(cache_k[l]), _cache_as_kd_slot(cache_v[l]),
        state_pool[l].transpose(1, 0, 2), relb, sinks, g_in, p_scale, g_out,
        w_in, pool_w_grp, w_br_pool, w_br_attn, w_out)
    weights = (g_in, w_in_bf, w_grp_bf, p_scale, w_brp_bf, w_bra_bf, w_out_bf, g_out)
    y_p, k_p, v_p, pool_p = _prompt_call(x_prompt, relb, sinks, weights)
    return (
        y_p,
        y_s,
        _cache_from_kd_slot(k_p)[None],
        _cache_from_kd_slot(v_p)[None],
        pool_p.transpose(1, 0, 2)[None],
        _cache_from_kd_slot(k_s)[None],
        _cache_from_kd_slot(v_s)[None],
        pool_s.transpose(1, 0, 2)[None],
    )
```

```python
import math

import jax
import jax.numpy as jnp
from jax import lax
from jax.experimental import pallas as pl
from jax.experimental.pallas import tpu as pltpu

D_MODEL = 1024
POOL_WINDOWS = (2, 4, 8, 16)
POOL_GROUPS = len(POOL_WINDOWS)
POOL_WIDTH = D_MODEL // 2
POOL_GROUP_DIM = POOL_WIDTH // POOL_GROUPS
POOL_STATE_ROWS = max(POOL_WINDOWS) - 1
HEAD_DIM = 64
N_KV_HEADS = 2
ATTN_WIDTH = D_MODEL // 2
N_Q_HEADS = ATTN_WIDTH // HEAD_DIM
GQA_GROUP = N_Q_HEADS // N_KV_HEADS
KV_WIDTH = N_KV_HEADS * HEAD_DIM
WINDOW = 128
N_BUCKETS = 32
MAX_DISTANCE = 128
RMS_EPS = 1e-6
IN_COLS = 2 * POOL_WIDTH + 2 * ATTN_WIDTH + 2 * KV_WIDTH + 2 * D_MODEL

OFF_U = 0
OFF_ZP = OFF_U + POOL_WIDTH
OFF_Q = OFF_ZP + POOL_WIDTH
OFF_K = OFF_Q + ATTN_WIDTH
OFF_V = OFF_K + KV_WIDTH
OFF_ZA = OFF_V + KV_WIDTH
OFF_GP = OFF_ZA + ATTN_WIDTH
OFF_GA = OFF_GP + D_MODEL

LANES = 128
SUBLANES = 8
POOL_HDR = 16
TM = 1024
BC = 16
W_STEPS = 4
OUT_CHUNKS = 4
VMEM_LIMIT_BYTES = 56 * 1024 * 1024

assert KV_WIDTH == LANES and 2 * HEAD_DIM == LANES and GQA_GROUP % 2 == 0

F32 = jnp.float32
BF16 = jnp.bfloat16
NEG_INF = float("-inf")


def _sigmoid(x):
    return 1.0 / (1.0 + jnp.exp(-x))


def _silu(x):
    return x * _sigmoid(x)


def _rmsnorm(x, g):
    return x * lax.rsqrt(jnp.mean(x * x, axis=-1, keepdims=True) + RMS_EPS) * g


def _dot(a, b):
    return jnp.dot(a, b, preferred_element_type=F32)


def _t5_bucket(rel):
    n = jnp.maximum(rel, 0)
    max_exact = N_BUCKETS // 2
    nf = jnp.maximum(n, 1).astype(F32)
    v = jnp.log(nf / max_exact) / math.log(MAX_DISTANCE / max_exact) * (N_BUCKETS - max_exact)
    steps = jnp.zeros(rel.shape, jnp.int32)
    for k in range(1, N_BUCKETS - max_exact):
        steps = steps + jnp.where(v >= k, 1, 0)
    return jnp.where(n < max_exact, n, max_exact + steps)


def _bias_table(rel, relb_ref, h):
    bucket = _t5_bucket(rel)

    def body(b, tbl):
        return jnp.where(bucket == b, relb_ref[h, b], tbl)

    tbl = lax.fori_loop(0, N_BUCKETS, body, jnp.zeros(rel.shape, F32))
    return jnp.where((rel >= 0) & (rel < WINDOW), tbl, NEG_INF)


def _per_head_column(ref_scalar):
    head = lax.broadcasted_iota(jnp.int32, (N_Q_HEADS, 1), 0)
    col = jnp.zeros((N_Q_HEADS, 1), F32)
    for h in range(N_Q_HEADS):
        col = jnp.where(head == h, ref_scalar(h), col)
    return col


def _project(xn_ref, w_in_ref, off, width):
    return _dot(xn_ref[...], w_in_ref[:, off:off + width])


def _kv_lane_mask(shape):
    return lax.broadcasted_iota(jnp.int32, shape, len(shape) - 1) < HEAD_DIM


def _head_pairs():
    for grp in range(ATTN_WIDTH // LANES):
        kv, pair = divmod(grp, GQA_GROUP // 2)
        yield grp, kv, kv * GQA_GROUP + 2 * pair, kv * GQA_GROUP + 2 * pair + 1


def _head_blocks(q):
    low = _kv_lane_mask((q.shape[0], LANES))
    blocks = [None] * N_Q_HEADS
    for grp, kv, even, odd in _head_pairs():
        a = q[:, grp * LANES:(grp + 1) * LANES]
        swapped = pltpu.roll(a, HEAD_DIM, 1)
        if kv == 0:
            blocks[even], blocks[odd] = jnp.where(low, a, 0.0), jnp.where(low, swapped, 0.0)
        else:
            blocks[even], blocks[odd] = jnp.where(low, 0.0, swapped), jnp.where(low, 0.0, a)
    return blocks


def _gather_heads(outs):
    low = _kv_lane_mask(outs[0].shape)
    groups = []
    for _, kv, even, odd in _head_pairs():
        if kv == 0:
            groups.append(jnp.where(low, outs[even], pltpu.roll(outs[odd], HEAD_DIM, 1)))
        else:
            groups.append(jnp.where(low, pltpu.roll(outs[even], HEAD_DIM, 1), outs[odd]))
    return jnp.concatenate(groups, axis=-1)


def _pool_windows(x_ext, inv_cnt):
    outs = []
    for g, w in enumerate(POOL_WINDOWS):
        xg = x_ext[:, g * POOL_GROUP_DIM:(g + 1) * POOL_GROUP_DIM]
        acc = xg
        shift = 1
        while shift < w:
            acc = acc + pltpu.roll(acc, shift, 0)
            shift *= 2
        outs.append(acc[POOL_HDR:] * inv_cnt[g] - xg[POOL_HDR:])
    return jnp.concatenate(outs, axis=-1)


def _merged_branches(pooled, zp_act, att_act, sig_gp, sig_ga, w_grp_ref, pool_scale, w_brp_ref, w_bra_ref):
    pg = jnp.concatenate(
        [_dot(pooled[:, g * POOL_GROUP_DIM:(g + 1) * POOL_GROUP_DIM], w_grp_ref[g])
         for g in range(POOL_GROUPS)], axis=-1)
    br_pool = _dot((pg * pool_scale * zp_act).astype(BF16), w_brp_ref[...])
    br_attn = _dot(att_act, w_bra_ref[...])
    return (sig_gp * br_pool + sig_ga * br_attn).astype(BF16)


def _prompt_kernel(relb_ref, sinks_ref, x_ref, gnorm_ref, w_in_ref, w_grp_ref, pscale_ref,
                   w_brp_ref, w_bra_ref, w_out_ref, gfinal_ref,
                   y_ref, knew_ref, vnew_ref, pnew_ref,
                   xn_s, uext_s, zp_s, qh_s, kband_s, vband_s, za_s, gp_s, ga_s, pooled_s, att_s, bias_s):
    b = pl.program_id(0)
    j = pl.program_id(1)
    nblk = TM // WINDOW
    hrows = N_Q_HEADS * WINDOW

    @pl.when((b == 0) & (j == 0))
    def _build_bias():
        qi = lax.broadcasted_iota(jnp.int32, (WINDOW, 2 * WINDOW), 0)
        kc = lax.broadcasted_iota(jnp.int32, (WINDOW, 2 * WINDOW), 1)
        for h in range(N_Q_HEADS):
            bias_s[h * WINDOW:(h + 1) * WINDOW, :] = _bias_table(qi - kc + WINDOW, relb_ref, h)

    @pl.when(j == 0)
    def _reset_carry():
        uext_s[0:POOL_HDR, :] = jnp.zeros((POOL_HDR, POOL_WIDTH), F32)
        kband_s[0:WINDOW, :] = jnp.zeros((WINDOW, KV_WIDTH), BF16)
        vband_s[0:WINDOW, :] = jnp.zeros((WINDOW, KV_WIDTH), BF16)

    def _tile():
        chunk = TM // OUT_CHUNKS
        for ch in range(OUT_CHUNKS):
            r = slice(ch * chunk, (ch + 1) * chunk)
            xn_s[r, :] = _rmsnorm(x_ref[0, r, :], gnorm_ref[...]).astype(BF16)
            uext_s[POOL_HDR + ch * chunk:POOL_HDR + (ch + 1) * chunk, :] = _dot(
                xn_s[r, :], w_in_ref[:, OFF_U:OFF_U + POOL_WIDTH])
        q_heads = _head_blocks(_project(xn_s, w_in_ref, OFF_Q, ATTN_WIDTH) * (HEAD_DIM ** -0.5))
        for c in range(nblk):
            for h in range(N_Q_HEADS):
                r0 = c * hrows + h * WINDOW
                qh_s[r0:r0 + WINDOW, :] = q_heads[h][c * WINDOW:(c + 1) * WINDOW].astype(BF16)

        pos = j * TM + lax.broadcasted_iota(jnp.int32, (TM, 1), 0)
        inv_cnt = [1.0 / jnp.minimum(pos + 1, w).astype(F32) for w in POOL_WINDOWS]
        x_ext = uext_s[...]
        pooled_s[...] = _pool_windows(x_ext, inv_cnt).astype(BF16)
        pnew_ref[:, pl.ds(b, 1), :] = x_ext[POOL_HDR + TM - POOL_STATE_ROWS:, :][:, None, :]
        uext_s[0:POOL_HDR, :] = x_ext[TM:, :]

        kv = _project(xn_s, w_in_ref, OFF_K, 2 * KV_WIDTH)
        kband_s[WINDOW:, :] = kv[:, :KV_WIDTH].astype(BF16)
        vband_s[WINDOW:, :] = kv[:, KV_WIDTH:].astype(BF16)
        knew_ref[0] = kv[TM - WINDOW:, :KV_WIDTH].T
        vnew_ref[0] = kv[TM - WINDOW:, KV_WIDTH:].T
        za_s[...] = _silu(_project(xn_s, w_in_ref, OFF_ZA, ATTN_WIDTH))

        first_neg = jnp.where(j == 0, NEG_INF, 0.0).astype(F32)
        for c in range(nblk):
            rows = slice(c * WINDOW, (c + 1) * WINDOW)
            k_c = kband_s[c * WINDOW:(c + 2) * WINDOW, :]
            v_c = vband_s[c * WINDOW:(c + 2) * WINDOW, :]
            s_all = lax.dot_general(qh_s[c * hrows:(c + 1) * hrows, :], k_c, (((1,), (1,)), ((), ())),
                                    preferred_element_type=F32)
            p_parts, inv_parts = [], []
            for h in range(N_Q_HEADS):
                head_rows = slice(h * WINDOW, (h + 1) * WINDOW)
                sc = s_all[head_rows, :] + bias_s[head_rows, :]
                if c == 0:
                    sc = jnp.concatenate([sc[:, :WINDOW] + first_neg, sc[:, WINDOW:]], axis=-1)
                sink = sinks_ref[h]
                m = jnp.maximum(jnp.max(sc, axis=-1, keepdims=True), sink)
                p = jnp.exp(sc - m)
                denom = jnp.sum(p, axis=-1, keepdims=True) + jnp.exp(sink - m)
                p_parts.append(p.astype(BF16))
                inv_parts.append(1.0 / denom)
            o_all = _dot(jnp.concatenate(p_parts, axis=0), v_c)
            outs = [o_all[h * WINDOW:(h + 1) * WINDOW, :] * inv_parts[h] for h in range(N_Q_HEADS)]
            att_s[rows, :] = (_gather_heads(outs) * za_s[rows, :]).astype(BF16)
        kband_s[0:WINDOW, :] = kband_s[TM:, :]
        vband_s[0:WINDOW, :] = vband_s[TM:, :]

        zp_s[...] = _silu(_project(xn_s, w_in_ref, OFF_ZP, POOL_WIDTH))
        for half in range(2):
            cols = slice(half * POOL_WIDTH, (half + 1) * POOL_WIDTH)
            gp_s[:, cols] = _sigmoid(_project(xn_s, w_in_ref, OFF_GP + half * POOL_WIDTH, POOL_WIDTH))
            ga_s[:, cols] = _sigmoid(_project(xn_s, w_in_ref, OFF_GA + half * POOL_WIDTH, POOL_WIDTH))

        merged = _merged_branches(pooled_s[...], zp_s[...], att_s[...], gp_s[...], ga_s[...],
                                  w_grp_ref, pscale_ref[...], w_brp_ref, w_bra_ref)
        for ch in range(OUT_CHUNKS):
            r = slice(ch * chunk, (ch + 1) * chunk)
            out = x_ref[0, r, :] + _dot(merged[r, :], w_out_ref[...])
            y_ref[0, r, :] = _rmsnorm(out, gfinal_ref[...])

    _tile()


def _resident(shape):
    return pl.BlockSpec(shape, lambda *_: (0,) * len(shape), pipeline_mode=pl.Buffered(1))


_SMEM = pl.BlockSpec(memory_space=pltpu.SMEM)


def _weight_specs():
    return [
        _resident((1, D_MODEL)),
        _resident((D_MODEL, IN_COLS)),
        _resident((POOL_GROUPS, POOL_GROUP_DIM, POOL_GROUP_DIM)),
        _resident((1, POOL_WIDTH)),
        _resident((POOL_WIDTH, D_MODEL)),
        _resident((ATTN_WIDTH, D_MODEL)),
        _resident((D_MODEL, D_MODEL)),
        _resident((1, D_MODEL)),
    ]


def _prompt_call(x, relb, sinks, weights):
    batch, seq, _ = x.shape
    assert seq % TM == 0 and TM % WINDOW == 0 and TM >= POOL_HDR
    tile_block = pl.BlockSpec((1, TM, D_MODEL), lambda b, j: (b, j, 0))

    def seq_block(shape):
        return pl.BlockSpec((1,) + shape, lambda b, j: (b, 0, 0))

    out_shape = (
        jax.ShapeDtypeStruct((batch, seq, D_MODEL), F32),
        jax.ShapeDtypeStruct((batch, KV_WIDTH, WINDOW), F32),
        jax.ShapeDtypeStruct((batch, KV_WIDTH, WINDOW), F32),
        jax.ShapeDtypeStruct((POOL_STATE_ROWS, batch, POOL_WIDTH), F32),
    )
    scratch = [
        pltpu.VMEM((TM, D_MODEL), BF16),
        pltpu.VMEM((POOL_HDR + TM, POOL_WIDTH), F32),
        pltpu.VMEM((TM, POOL_WIDTH), F32),
        pltpu.VMEM((TM * N_Q_HEADS, LANES), BF16),
        pltpu.VMEM((WINDOW + TM, KV_WIDTH), BF16),
        pltpu.VMEM((WINDOW + TM, KV_WIDTH), BF16),
        pltpu.VMEM((TM, ATTN_WIDTH), F32),
        pltpu.VMEM((TM, D_MODEL), F32),
        pltpu.VMEM((TM, D_MODEL), F32),
        pltpu.VMEM((TM, POOL_WIDTH), BF16),
        pltpu.VMEM((TM, ATTN_WIDTH), BF16),
        pltpu.VMEM((N_Q_HEADS * WINDOW, 2 * WINDOW), F32),
    ]
    return pl.pallas_call(
        _prompt_kernel,
        grid=(batch, seq // TM),
        in_specs=[_SMEM, _SMEM, tile_block] + _weight_specs(),
        out_specs=(
            tile_block,
            seq_block((KV_WIDTH, WINDOW)),
            seq_block((KV_WIDTH, WINDOW)),
            pl.BlockSpec((POOL_STATE_ROWS, batch, POOL_WIDTH), lambda b, j: (0, 0, 0)),
        ),
        out_shape=out_shape,
        scratch_shapes=scratch,
        compiler_params=pltpu.CompilerParams(
            dimension_semantics=("arbitrary", "arbitrary"), vmem_limit_bytes=VMEM_LIMIT_BYTES),
        name="prompt_layer",
    )(relb, sinks, x, *weights)


def _last_slot_columns(rows, place_ref):
    t = rows.T
    hi = t.astype(BF16)
    rest = t - hi.astype(F32)
    mid = rest.astype(BF16)
    lo = (rest - mid.astype(F32)).astype(BF16)
    return _dot(jnp.concatenate([hi, mid, lo], axis=1), place_ref[...])


def _placement_matrix():
    r = lax.broadcasted_iota(jnp.int32, (3 * BC, BC * WINDOW), 0)
    c = lax.broadcasted_iota(jnp.int32, (3 * BC, BC * WINDOW), 1)
    hit = (c == (r % BC) * WINDOW + WINDOW - 1)
    return jnp.where(hit, 1.0, 0.0).astype(BF16)


def _sample_kernel(relb_ref, sinks_ref, x_ref, kt_ref, vt_ref, st_ref, gnorm_ref, pscale_ref, gfinal_ref,
                   w_in_ref, w_grp_ref, w_brp_ref, w_bra_ref, w_out_ref,
                   y_ref, knew_ref, vnew_ref, pnew_ref, o_in_ref, o_grp_ref, o_brp_ref, o_bra_ref, o_out_ref,
                   xn_s, h_s, u_s, zp_s, qh_s, k_s, v_s, za_s, gp_s, ga_s, pooled_s, att_s,
                   q2_s, kn2_s, vn2_s, o2_s, bias_s, place_s, wgrp_s, wbrp_s, wbra_s, wout_s):
    s = pl.program_id(0)
    n_steps = pl.num_programs(0)
    n_seq = x_ref.shape[0]
    rows_in = D_MODEL // W_STEPS

    @pl.when(s == 0)
    def _normalise():
        xn = _rmsnorm(x_ref[:, 0, :], gnorm_ref[...]).astype(BF16)
        for c in range(W_STEPS):
            xn_s[c] = xn[:, c * rows_in:(c + 1) * rows_in]
        h_s[...] = jnp.zeros((n_seq, IN_COLS), F32)

    @pl.when(s < W_STEPS)
    def _weight_chunk():
        w_in_bf = w_in_ref[0].astype(BF16)
        o_in_ref[...] = w_in_bf
        h_s[...] += _dot(xn_s[s], w_in_bf)
        for w_ref, o_ref, copy_s, axis in ((w_grp_ref, o_grp_ref, wgrp_s, 1), (w_brp_ref, o_brp_ref, wbrp_s, 0),
                                          (w_bra_ref, o_bra_ref, wbra_s, 0), (w_out_ref, o_out_ref, wout_s, 0)):
            w_bf = w_ref[0].astype(BF16)
            o_ref[...] = w_bf
            rows = w_bf.shape[axis]
            chunk_rows = pl.ds(pl.multiple_of(s * rows, rows), rows)
            if axis == 0:
                copy_s[chunk_rows, :] = w_bf
            else:
                copy_s[:, chunk_rows, :] = w_bf

    @pl.when(s == W_STEPS - 1)
    def _activations():
        slot = lax.broadcasted_iota(jnp.int32, (SUBLANES, WINDOW), 1)
        head = lax.broadcasted_iota(jnp.int32, (SUBLANES, WINDOW), 0)
        tbl = jnp.zeros((SUBLANES, WINDOW), F32)
        for h in range(N_Q_HEADS):
            tbl = jnp.where(head == h, _bias_table(WINDOW - slot, relb_ref, h), tbl)
        bias_s[...] = tbl
        place_s[...] = _placement_matrix()

        u_s[...] = h_s[:, OFF_U:OFF_U + POOL_WIDTH]
        zp_s[...] = _silu(h_s[:, OFF_ZP:OFF_ZP + POOL_WIDTH])
        q_heads = _head_blocks(h_s[:, OFF_Q:OFF_Q + ATTN_WIDTH] * (HEAD_DIM ** -0.5))
        for h in range(N_Q_HEADS):
            qh_s[:, h * LANES:(h + 1) * LANES] = q_heads[h]
        k_s[...] = h_s[:, OFF_K:OFF_K + KV_WIDTH]
        v_s[...] = h_s[:, OFF_V:OFF_V + KV_WIDTH]
        za_s[...] = _silu(h_s[:, OFF_ZA:OFF_ZA + ATTN_WIDTH])
        gp_s[...] = _sigmoid(h_s[:, OFF_GP:OFF_GP + D_MODEL])
        ga_s[...] = _sigmoid(h_s[:, OFF_GA:OFF_GA + D_MODEL])

    @pl.when(s >= W_STEPS)
    def _sequence_chunk():
        i = s - W_STEPS
        chunk = pl.ds(pl.multiple_of(i * BC, BC), BC)

        u_new = u_s[chunk, :]
        n_hist = POOL_STATE_ROWS
        parts = []
        for g, w in enumerate(POOL_WINDOWS):
            cols = slice(g * POOL_GROUP_DIM, (g + 1) * POOL_GROUP_DIM)
            acc = u_new[:, cols]
            for r in range(n_hist - (w - 1), n_hist):
                acc = acc + st_ref[r, :, cols]
            parts.append(acc * (1.0 / w) - u_new[:, cols])
        pooled_s[chunk, :] = jnp.concatenate(parts, axis=-1).astype(BF16)
        pnew_ref[0:n_hist - 1] = st_ref[1:n_hist]
        pnew_ref[n_hist - 1] = u_new

        k_new = k_s[chunk, :]
        v_new = v_s[chunk, :]
        last_slot = lax.broadcasted_iota(jnp.int32, (KV_WIDTH, WINDOW), 1) == WINDOW - 1
        k_cols = _last_slot_columns(k_new, place_s)
        v_cols = _last_slot_columns(v_new, place_s)
        for b in range(BC):
            lanes = slice(b * WINDOW, (b + 1) * WINDOW)
            knew_ref[b] = jnp.where(last_slot, k_cols[:, lanes], pltpu.roll(kt_ref[b], WINDOW - 1, 1))
            vnew_ref[b] = jnp.where(last_slot, v_cols[:, lanes], pltpu.roll(vt_ref[b], WINDOW - 1, 1))

        for h in range(N_Q_HEADS):
            q2_s[pl.ds(h, BC, stride=N_Q_HEADS), :] = qh_s[chunk, h * LANES:(h + 1) * LANES]
            kn2_s[pl.ds(h, BC, stride=N_Q_HEADS), :] = k_new
            vn2_s[pl.ds(h, BC, stride=N_Q_HEADS), :] = v_new
        q3 = q2_s[...].reshape(BC, N_Q_HEADS, LANES)
        kn3 = kn2_s[...].reshape(BC, N_Q_HEADS, LANES)
        vn3 = vn2_s[...].reshape(BC, N_Q_HEADS, LANES)
        s3 = lax.dot_general(q3.astype(BF16), kt_ref[...].astype(BF16), (((2,), (1,)), ((0,), (0,))),
                             preferred_element_type=F32) + bias_s[...][None]
        s_self = jnp.sum(q3 * kn3, axis=-1, keepdims=True) + _per_head_column(lambda h: relb_ref[h, 0])[None]
        sink = _per_head_column(lambda h: sinks_ref[h])[None]
        m = jnp.maximum(jnp.maximum(jnp.max(s3, axis=-1, keepdims=True), s_self), sink)
        p3 = jnp.exp(s3 - m)
        p_self = jnp.exp(s_self - m)
        denom = jnp.sum(p3, axis=-1, keepdims=True) + p_self + jnp.exp(sink - m)
        o3 = lax.dot_general(p3.astype(BF16), vt_ref[...].astype(BF16), (((2,), (2,)), ((0,), (0,))),
                             preferred_element_type=F32)
        o3 = (o3 + p_self * vn3) * (1.0 / denom)
        o2_s[...] = o3.reshape(BC * N_Q_HEADS, LANES)
        outs = [o2_s[pl.ds(h, BC, stride=N_Q_HEADS), :] for h in range(N_Q_HEADS)]
        att_s[chunk, :] = (_gather_heads(outs) * za_s[chunk, :]).astype(BF16)

    @pl.when(s == n_steps - 1)
    def _merge_all():
        merged = _merged_branches(pooled_s[...], zp_s[...], att_s[...], gp_s[...], ga_s[...],
                                  wgrp_s, pscale_ref[...], wbrp_s, wbra_s)
        out = x_ref[:, 0, :] + _dot(merged, wout_s[...])
        y_ref[:, 0, :] = _rmsnorm(out, gfinal_ref[...])


def _sample_call(x, cache_kt, cache_vt, state_t, relb, sinks, g_norm, pool_scale, g_final,
                 w_in, w_grp, w_brp, w_bra, w_out):
    n_seq = x.shape[0]
    assert n_seq % BC == 0

    def chunk_rows(n):
        assert n % (W_STEPS * 2 * SUBLANES) == 0
        return n // W_STEPS

    r_in, r_grp, r_br, r_out = (chunk_rows(D_MODEL), chunk_rows(POOL_GROUP_DIM), chunk_rows(POOL_WIDTH),
                                chunk_rows(D_MODEL))

    def w_step(s):
        return jnp.minimum(s, W_STEPS - 1)

    def seq_step(s):
        return jnp.maximum(s - W_STEPS, 0)

    weight_in_specs = [
        pl.BlockSpec((1, r_in, IN_COLS), lambda s: (0, w_step(s), 0)),
        pl.BlockSpec((1, POOL_GROUPS, r_grp, POOL_GROUP_DIM), lambda s: (0, 0, w_step(s), 0)),
        pl.BlockSpec((1, r_br, D_MODEL), lambda s: (0, w_step(s), 0)),
        pl.BlockSpec((1, r_br, D_MODEL), lambda s: (0, w_step(s), 0)),
        pl.BlockSpec((1, r_out, D_MODEL), lambda s: (0, w_step(s), 0)),
    ]
    weight_out_specs = (
        pl.BlockSpec((r_in, IN_COLS), lambda s: (w_step(s), 0)),
        pl.BlockSpec((POOL_GROUPS, r_grp, POOL_GROUP_DIM), lambda s: (0, w_step(s), 0)),
        pl.BlockSpec((r_br, D_MODEL), lambda s: (w_step(s), 0)),
        pl.BlockSpec((r_br, D_MODEL), lambda s: (w_step(s), 0)),
        pl.BlockSpec((r_out, D_MODEL), lambda s: (w_step(s), 0)),
    )
    out_shape = (
        jax.ShapeDtypeStruct((n_seq, 1, D_MODEL), F32),
        jax.ShapeDtypeStruct((n_seq, KV_WIDTH, WINDOW), F32),
        jax.ShapeDtypeStruct((n_seq, KV_WIDTH, WINDOW), F32),
        jax.ShapeDtypeStruct((POOL_STATE_ROWS, n_seq, POOL_WIDTH), F32),
        jax.ShapeDtypeStruct((D_MODEL, IN_COLS), BF16),
        jax.ShapeDtypeStruct((POOL_GROUPS, POOL_GROUP_DIM, POOL_GROUP_DIM), BF16),
        jax.ShapeDtypeStruct((POOL_WIDTH, D_MODEL), BF16),
        jax.ShapeDtypeStruct((ATTN_WIDTH, D_MODEL), BF16),
        jax.ShapeDtypeStruct((D_MODEL, D_MODEL), BF16),
    )
    scratch = [
        pltpu.VMEM((W_STEPS, n_seq, D_MODEL // W_STEPS), BF16),
        pltpu.VMEM((n_seq, IN_COLS), F32),
        pltpu.VMEM((n_seq, POOL_WIDTH), F32),
        pltpu.VMEM((n_seq, POOL_WIDTH), F32),
        pltpu.VMEM((n_seq, N_Q_HEADS * LANES), F32),
        pltpu.VMEM((n_seq, KV_WIDTH), F32),
        pltpu.VMEM((n_seq, KV_WIDTH), F32),
        pltpu.VMEM((n_seq, ATTN_WIDTH), F32),
        pltpu.VMEM((n_seq, D_MODEL), F32),
        pltpu.VMEM((n_seq, D_MODEL), F32),
        pltpu.VMEM((n_seq, POOL_WIDTH), BF16),
        pltpu.VMEM((n_seq, ATTN_WIDTH), BF16),
        pltpu.VMEM((BC * N_Q_HEADS, LANES), F32),
        pltpu.VMEM((BC * N_Q_HEADS, LANES), F32),
        pltpu.VMEM((BC * N_Q_HEADS, LANES), F32),
        pltpu.VMEM((BC * N_Q_HEADS, LANES), F32),
        pltpu.VMEM((SUBLANES, WINDOW), F32),
        pltpu.VMEM((3 * BC, BC * WINDOW), BF16),
        pltpu.VMEM((POOL_GROUPS, POOL_GROUP_DIM, POOL_GROUP_DIM), BF16),
        pltpu.VMEM((POOL_WIDTH, D_MODEL), BF16),
        pltpu.VMEM((ATTN_WIDTH, D_MODEL), BF16),
        pltpu.VMEM((D_MODEL, D_MODEL), BF16),
    ]
    cache = pl.BlockSpec((BC, KV_WIDTH, WINDOW), lambda s: (seq_step(s), 0, 0))
    hist = pl.BlockSpec((POOL_STATE_ROWS, BC, POOL_WIDTH), lambda s: (0, seq_step(s), 0))
    return pl.pallas_call(
        _sample_kernel,
        grid=(W_STEPS + n_seq // BC,),
        in_specs=[_SMEM, _SMEM, _resident((n_seq, 1, D_MODEL)), cache, cache, hist,
                  _resident((1, D_MODEL)), _resident((1, POOL_WIDTH)), _resident((1, D_MODEL))] + weight_in_specs,
        out_specs=(pl.BlockSpec((n_seq, 1, D_MODEL), lambda s: (0, 0, 0)), cache, cache, hist) + weight_out_specs,
        out_shape=out_shape,
        scratch_shapes=scratch,
        compiler_params=pltpu.CompilerParams(
            dimension_semantics=("arbitrary",), vmem_limit_bytes=VMEM_LIMIT_BYTES),
        name="sample_layer",
    )(relb, sinks, x, cache_kt, cache_vt, state_t, g_norm, pool_scale, g_final, w_in, w_grp, w_brp, w_bra, w_out)


def _cache_as_kd_slot(cache):
    n = cache.shape[0]
    return cache.transpose(0, 2, 3, 1).reshape(n, KV_WIDTH, WINDOW)


def _cache_from_kd_slot(cache_t):
    n = cache_t.shape[0]
    return cache_t.reshape(n, N_KV_HEADS, HEAD_DIM, WINDOW).transpose(0, 3, 1, 2)


def kernel(x_prompt, x_sample, cache_k, cache_v, state_pool, rel_bias, g_norm, w_in, pool_w_grp, pool_scale, attn_sinks, w_br_pool, w_br_attn, w_out, g_final):
    depth = g_norm.shape[0]
    assert depth == 1 and x_sample.shape[1] == 1
    l = 0
    sinks = attn_sinks[l]
    relb = rel_bias.T
    g_in, p_scale, g_out = g_norm[l].reshape(1, D_MODEL), pool_scale[l].reshape(1, POOL_WIDTH), g_final.reshape(1, D_MODEL)

    y_s, k_s, v_s, pool_s, w_in_bf, w_grp_bf, w_brp_bf, w_bra_bf, w_out_bf = _sample_call(
        x_sample, _cache_as_kd_slot(cache_k[l]), _cache_as_kd_slot(cache_v[l]),
        state_pool[l].transpose(1, 0, 2), relb, sinks, g_in, p_scale, g_out,
        w_in, pool_w_grp, w_br_pool, w_br_attn, w_out)
    weights = (g_in, w_in_bf, w_grp_bf, p_scale, w_brp_bf, w_bra_bf, w_out_bf, g_out)
    y_p, k_p, v_p, pool_p = _prompt_call(x_prompt, relb, sinks, weights)
    return (
        y_p,
        y_s,
        _cache_from_kd_slot(k_p)[None],
        _cache_from_kd_slot(v_p)[None],
        pool_p.transpose(1, 0, 2)[None],
        _cache_from_kd_slot(k_s)[None],
        _cache_from_kd_slot(v_s)[None],
        pool_s.transpose(1, 0, 2)[None],
    )
```

```python
import math

import jax
import jax.numpy as jnp
from jax import lax
from jax.experimental import pallas as pl
from jax.experimental.pallas import tpu as pltpu

D_MODEL = 1024
POOL_WINDOWS = (2, 4, 8, 16)
POOL_GROUPS = len(POOL_WINDOWS)
POOL_WIDTH = D_MODEL // 2
POOL_GROUP_DIM = POOL_WIDTH // POOL_GROUPS
POOL_STATE_ROWS = max(POOL_WINDOWS) - 1
HEAD_DIM = 64
N_KV_HEADS = 2
ATTN_WIDTH = D_MODEL // 2
N_Q_HEADS = ATTN_WIDTH // HEAD_DIM
GQA_GROUP = N_Q_HEADS // N_KV_HEADS
KV_WIDTH = N_KV_HEADS * HEAD_DIM
WINDOW = 128
N_BUCKETS = 32
MAX_DISTANCE = 128
RMS_EPS = 1e-6
IN_COLS = 2 * POOL_WIDTH + 2 * ATTN_WIDTH + 2 * KV_WIDTH + 2 * D_MODEL

OFF_U = 0
OFF_ZP = OFF_U + POOL_WIDTH
OFF_Q = OFF_ZP + POOL_WIDTH
OFF_K = OFF_Q + ATTN_WIDTH
OFF_V = OFF_K + KV_WIDTH
OFF_ZA = OFF_V + KV_WIDTH
OFF_GP = OFF_ZA + ATTN_WIDTH
OFF_GA = OFF_GP + D_MODEL

LANES = 128
SUBLANES = 8
POOL_HDR = 16
TM = 1024
BC = 32
W_STEPS = 4
OUT_CHUNKS = 4
VMEM_LIMIT_BYTES = 56 * 1024 * 1024

assert KV_WIDTH == LANES and 2 * HEAD_DIM == LANES and GQA_GROUP % 2 == 0

F32 = jnp.float32
BF16 = jnp.bfloat16
NEG_INF = float("-inf")


def _sigmoid(x):
    return 1.0 / (1.0 + jnp.exp(-x))


def _silu(x):
    return x * _sigmoid(x)


def _rmsnorm(x, g):
    return x * lax.rsqrt(jnp.mean(x * x, axis=-1, keepdims=True) + RMS_EPS) * g


def _dot(a, b):
    return jnp.dot(a, b, preferred_element_type=F32)


def _t5_bucket(rel):
    n = jnp.maximum(rel, 0)
    max_exact = N_BUCKETS // 2
    nf = jnp.maximum(n, 1).astype(F32)
    v = jnp.log(nf / max_exact) / math.log(MAX_DISTANCE / max_exact) * (N_BUCKETS - max_exact)
    steps = jnp.zeros(rel.shape, jnp.int32)
    for k in range(1, N_BUCKETS - max_exact):
        steps = steps + jnp.where(v >= k, 1, 0)
    return jnp.where(n < max_exact, n, max_exact + steps)


def _bias_table(rel, relb_ref, h):
    bucket = _t5_bucket(rel)

    def body(b, tbl):
        return jnp.where(bucket == b, relb_ref[h, b], tbl)

    tbl = lax.fori_loop(0, N_BUCKETS, body, jnp.zeros(rel.shape, F32))
    return jnp.where((rel >= 0) & (rel < WINDOW), tbl, NEG_INF)


def _per_head_column(ref_scalar):
    head = lax.broadcasted_iota(jnp.int32, (N_Q_HEADS, 1), 0)
    col = jnp.zeros((N_Q_HEADS, 1), F32)
    for h in range(N_Q_HEADS):
        col = jnp.where(head == h, ref_scalar(h), col)
    return col


def _project(xn_ref, w_in_ref, off, width):
    return _dot(xn_ref[...], w_in_ref[:, off:off + width])


def _kv_lane_mask(shape):
    return lax.broadcasted_iota(jnp.int32, shape, len(shape) - 1) < HEAD_DIM


def _head_pairs():
    for grp in range(ATTN_WIDTH // LANES):
        kv, pair = divmod(grp, GQA_GROUP // 2)
        yield grp, kv, kv * GQA_GROUP + 2 * pair, kv * GQA_GROUP + 2 * pair + 1


def _head_blocks(q):
    low = _kv_lane_mask((q.shape[0], LANES))
    blocks = [None] * N_Q_HEADS
    for grp, kv, even, odd in _head_pairs():
        a = q[:, grp * LANES:(grp + 1) * LANES]
        swapped = pltpu.roll(a, HEAD_DIM, 1)
        if kv == 0:
            blocks[even], blocks[odd] = jnp.where(low, a, 0.0), jnp.where(low, swapped, 0.0)
        else:
            blocks[even], blocks[odd] = jnp.where(low, 0.0, swapped), jnp.where(low, 0.0, a)
    return blocks


def _gather_heads(outs):
    low = _kv_lane_mask(outs[0].shape)
    groups = []
    for _, kv, even, odd in _head_pairs():
        if kv == 0:
            groups.append(jnp.where(low, outs[even], pltpu.roll(outs[odd], HEAD_DIM, 1)))
        else:
            groups.append(jnp.where(low, pltpu.roll(outs[even], HEAD_DIM, 1), outs[odd]))
    return jnp.concatenate(groups, axis=-1)


def _pool_windows(x_ext, inv_cnt):
    outs = []
    for g, w in enumerate(POOL_WINDOWS):
        xg = x_ext[:, g * POOL_GROUP_DIM:(g + 1) * POOL_GROUP_DIM]
        acc = xg
        shift = 1
        while shift < w:
            acc = acc + pltpu.roll(acc, shift, 0)
            shift *= 2
        outs.append(acc[POOL_HDR:] * inv_cnt[g] - xg[POOL_HDR:])
    return jnp.concatenate(outs, axis=-1)


def _merged_branches(pooled, zp_act, att_act, sig_gp, sig_ga, w_grp_ref, pool_scale, w_brp_ref, w_bra_ref):
    pg = jnp.concatenate(
        [_dot(pooled[:, g * POOL_GROUP_DIM:(g + 1) * POOL_GROUP_DIM], w_grp_ref[g])
         for g in range(POOL_GROUPS)], axis=-1)
    br_pool = _dot((pg * pool_scale * zp_act).astype(BF16), w_brp_ref[...])
    br_attn = _dot(att_act, w_bra_ref[...])
    return (sig_gp * br_pool + sig_ga * br_attn).astype(BF16)


def _prompt_step(step, relb_ref, sinks_ref, x_ref, gnorm_ref, w_in_ref, w_grp_ref, pscale_ref,
                 w_brp_ref, w_bra_ref, w_out_ref, gfinal_ref,
                 y_ref, knew_ref, vnew_ref, pnew_ref,
                 xn_s, uext_s, zp_s, qh_s, kband_s, vband_s, za_s, gp_s, ga_s, pooled_s, att_s, bias_s):
    b, j = step if isinstance(step, tuple) else step.index
    nblk = TM // WINDOW
    hrows = N_Q_HEADS * WINDOW

    @pl.when((b == 0) & (j == 0))
    def _build_bias():
        qi = lax.broadcasted_iota(jnp.int32, (WINDOW, 2 * WINDOW), 0)
        kc = lax.broadcasted_iota(jnp.int32, (WINDOW, 2 * WINDOW), 1)
        for h in range(N_Q_HEADS):
            bias_s[h * WINDOW:(h + 1) * WINDOW, :] = _bias_table(qi - kc + WINDOW, relb_ref, h)

    @pl.when(j == 0)
    def _reset_carry():
        uext_s[0:POOL_HDR, :] = jnp.zeros((POOL_HDR, POOL_WIDTH), F32)
        kband_s[0:WINDOW, :] = jnp.zeros((WINDOW, KV_WIDTH), BF16)
        vband_s[0:WINDOW, :] = jnp.zeros((WINDOW, KV_WIDTH), BF16)

    def _tile():
        chunk = TM // OUT_CHUNKS
        for ch in range(OUT_CHUNKS):
            r = slice(ch * chunk, (ch + 1) * chunk)
            xn_s[r, :] = _rmsnorm(x_ref[0, r, :], gnorm_ref[...]).astype(BF16)
            uext_s[POOL_HDR + ch * chunk:POOL_HDR + (ch + 1) * chunk, :] = _dot(
                xn_s[r, :], w_in_ref[:, OFF_U:OFF_U + POOL_WIDTH])
        q_heads = _head_blocks(_project(xn_s, w_in_ref, OFF_Q, ATTN_WIDTH) * (HEAD_DIM ** -0.5))
        for c in range(nblk):
            for h in range(N_Q_HEADS):
                r0 = c * hrows + h * WINDOW
                qh_s[r0:r0 + WINDOW, :] = q_heads[h][c * WINDOW:(c + 1) * WINDOW].astype(BF16)

        pos = j * TM + lax.broadcasted_iota(jnp.int32, (TM, 1), 0)
        inv_cnt = [1.0 / jnp.minimum(pos + 1, w).astype(F32) for w in POOL_WINDOWS]
        x_ext = uext_s[...]
        pooled_s[...] = _pool_windows(x_ext, inv_cnt).astype(BF16)
        pnew_ref[:, pl.ds(b, 1), :] = x_ext[POOL_HDR + TM - POOL_STATE_ROWS:, :][:, None, :]
        uext_s[0:POOL_HDR, :] = x_ext[TM:, :]

        kv = _project(xn_s, w_in_ref, OFF_K, 2 * KV_WIDTH)
        kband_s[WINDOW:, :] = kv[:, :KV_WIDTH].astype(BF16)
        vband_s[WINDOW:, :] = kv[:, KV_WIDTH:].astype(BF16)
        knew_ref[0] = kv[TM - WINDOW:, :KV_WIDTH].T
        vnew_ref[0] = kv[TM - WINDOW:, KV_WIDTH:].T
        za_s[...] = _silu(_project(xn_s, w_in_ref, OFF_ZA, ATTN_WIDTH))

        first_neg = jnp.where(j == 0, NEG_INF, 0.0).astype(F32)
        for c in range(nblk):
            rows = slice(c * WINDOW, (c + 1) * WINDOW)
            k_c = kband_s[c * WINDOW:(c + 2) * WINDOW, :]
            v_c = vband_s[c * WINDOW:(c + 2) * WINDOW, :]
            s_all = lax.dot_general(qh_s[c * hrows:(c + 1) * hrows, :], k_c, (((1,), (1,)), ((), ())),
                                    preferred_element_type=F32)
            p_parts, inv_parts = [], []
            for h in range(N_Q_HEADS):
                head_rows = slice(h * WINDOW, (h + 1) * WINDOW)
                sc = s_all[head_rows, :] + bias_s[head_rows, :]
                if c == 0:
                    sc = jnp.concatenate([sc[:, :WINDOW] + first_neg, sc[:, WINDOW:]], axis=-1)
                sink = sinks_ref[h]
                m = jnp.maximum(jnp.max(sc, axis=-1, keepdims=True), sink)
                p = jnp.exp(sc - m)
                denom = jnp.sum(p, axis=-1, keepdims=True) + jnp.exp(sink - m)
                p_parts.append(p.astype(BF16))
                inv_parts.append(1.0 / denom)
            o_all = _dot(jnp.concatenate(p_parts, axis=0), v_c)
            outs = [o_all[h * WINDOW:(h + 1) * WINDOW, :] * inv_parts[h] for h in range(N_Q_HEADS)]
            att_s[rows, :] = (_gather_heads(outs) * za_s[rows, :]).astype(BF16)
        kband_s[0:WINDOW, :] = kband_s[TM:, :]
        vband_s[0:WINDOW, :] = vband_s[TM:, :]

        zp_s[...] = _silu(_project(xn_s, w_in_ref, OFF_ZP, POOL_WIDTH))
        for half in range(2):
            cols = slice(half * POOL_WIDTH, (half + 1) * POOL_WIDTH)
            gp_s[:, cols] = _sigmoid(_project(xn_s, w_in_ref, OFF_GP + half * POOL_WIDTH, POOL_WIDTH))
            ga_s[:, cols] = _sigmoid(_project(xn_s, w_in_ref, OFF_GA + half * POOL_WIDTH, POOL_WIDTH))

        merged = _merged_branches(pooled_s[...], zp_s[...], att_s[...], gp_s[...], ga_s[...],
                                  w_grp_ref, pscale_ref[...], w_brp_ref, w_bra_ref)
        for ch in range(OUT_CHUNKS):
            r = slice(ch * chunk, (ch + 1) * chunk)
            out = x_ref[0, r, :] + _dot(merged[r, :], w_out_ref[...])
            y_ref[0, r, :] = _rmsnorm(out, gfinal_ref[...])

    _tile()


def _resident(shape):
    return pl.BlockSpec(shape, lambda *_: (0,) * len(shape), pipeline_mode=pl.Buffered(1))


_SMEM = pl.BlockSpec(memory_space=pltpu.SMEM)


def _weight_specs():
    return [
        _resident((1, D_MODEL)),
        _resident((D_MODEL, IN_COLS)),
        _resident((POOL_GROUPS, POOL_GROUP_DIM, POOL_GROUP_DIM)),
        _resident((1, POOL_WIDTH)),
        _resident((POOL_WIDTH, D_MODEL)),
        _resident((ATTN_WIDTH, D_MODEL)),
        _resident((D_MODEL, D_MODEL)),
        _resident((1, D_MODEL)),
    ]


def _prompt_call(x, relb, sinks, weights):
    batch, seq, _ = x.shape
    assert seq % TM == 0 and TM % WINDOW == 0 and TM >= POOL_HDR
    tile_block = pl.BlockSpec((1, TM, D_MODEL), lambda b, j: (b, j, 0))

    def seq_block(shape):
        return pl.BlockSpec((1,) + shape, lambda b, j: (b, 0, 0))

    out_shape = (
        jax.ShapeDtypeStruct((batch, seq, D_MODEL), F32),
        jax.ShapeDtypeStruct((batch, KV_WIDTH, WINDOW), F32),
        jax.ShapeDtypeStruct((batch, KV_WIDTH, WINDOW), F32),
        jax.ShapeDtypeStruct((POOL_STATE_ROWS, batch, POOL_WIDTH), F32),
    )
    scratch = [
        pltpu.VMEM((TM, D_MODEL), BF16),
        pltpu.VMEM((POOL_HDR + TM, POOL_WIDTH), F32),
        pltpu.VMEM((TM, POOL_WIDTH), F32),
        pltpu.VMEM((TM * N_Q_HEADS, LANES), BF16),
        pltpu.VMEM((WINDOW + TM, KV_WIDTH), BF16),
        pltpu.VMEM((WINDOW + TM, KV_WIDTH), BF16),
        pltpu.VMEM((TM, ATTN_WIDTH), F32),
        pltpu.VMEM((TM, D_MODEL), F32),
        pltpu.VMEM((TM, D_MODEL), F32),
        pltpu.VMEM((TM, POOL_WIDTH), BF16),
        pltpu.VMEM((TM, ATTN_WIDTH), BF16),
        pltpu.VMEM((N_Q_HEADS * WINDOW, 2 * WINDOW), F32),
    ]
    tile_out_specs = [
        tile_block,
        seq_block((KV_WIDTH, WINDOW)),
        seq_block((KV_WIDTH, WINDOW)),
        pl.BlockSpec((POOL_STATE_ROWS, batch, POOL_WIDTH), lambda b, j: (0, 0, 0)),
    ]
    n_weights = len(weights)

    def outer(relb_ref, sinks_ref, x_hbm, *rest):
        weight_refs, out_hbm, scratch_refs = rest[:n_weights], rest[n_weights:n_weights + 4], rest[n_weights + 4:]

        def body(step, x_ref, y_ref, knew_ref, vnew_ref, pnew_ref):
            _prompt_step(step, relb_ref, sinks_ref, x_ref, *weight_refs, y_ref, knew_ref, vnew_ref, pnew_ref,
                         *scratch_refs)

        pltpu.emit_pipeline(body, grid=(batch, seq // TM), in_specs=[tile_block], out_specs=tile_out_specs,
                            _explicit_indices=True)(x_hbm, *out_hbm)

    any_space = pl.BlockSpec(memory_space=pl.ANY)
    vmem_whole = pl.BlockSpec(memory_space=pltpu.VMEM)
    return pl.pallas_call(
        outer,
        in_specs=[_SMEM, _SMEM, any_space] + [vmem_whole] * n_weights,
        out_specs=(any_space,) * 4,
        out_shape=out_shape,
        scratch_shapes=scratch,
        compiler_params=pltpu.CompilerParams(vmem_limit_bytes=VMEM_LIMIT_BYTES),
        name="prompt_layer",
    )(relb, sinks, x, *weights)


def _last_slot_columns(rows, place_ref):
    t = rows.T
    hi = t.astype(BF16)
    rest = t - hi.astype(F32)
    mid = rest.astype(BF16)
    lo = (rest - mid.astype(F32)).astype(BF16)
    return _dot(jnp.concatenate([hi, mid, lo], axis=1), place_ref[...])


def _placement_matrix():
    r = lax.broadcasted_iota(jnp.int32, (3 * BC, BC * WINDOW), 0)
    c = lax.broadcasted_iota(jnp.int32, (3 * BC, BC * WINDOW), 1)
    hit = (c == (r % BC) * WINDOW + WINDOW - 1)
    return jnp.where(hit, 1.0, 0.0).astype(BF16)


def _sample_kernel(relb_ref, sinks_ref, x_ref, kt_ref, vt_ref, st_ref, gnorm_ref, pscale_ref, gfinal_ref,
                   w_in_ref, w_grp_ref, w_brp_ref, w_bra_ref, w_out_ref,
                   y_ref, knew_ref, vnew_ref, pnew_ref, o_in_ref, o_grp_ref, o_brp_ref, o_bra_ref, o_out_ref,
                   xn_s, h_s, u_s, zp_s, qh_s, k_s, v_s, za_s, gp_s, ga_s, pooled_s, att_s,
                   q2_s, kn2_s, vn2_s, o2_s, bias_s, place_s, wgrp_s, wbrp_s, wbra_s, wout_s):
    s = pl.program_id(0)
    n_steps = pl.num_programs(0)
    n_seq = x_ref.shape[0]
    rows_in = D_MODEL // W_STEPS

    @pl.when(s == 0)
    def _normalise():
        xn = _rmsnorm(x_ref[:, 0, :], gnorm_ref[...]).astype(BF16)
        for c in range(W_STEPS):
            xn_s[c] = xn[:, c * rows_in:(c + 1) * rows_in]
        h_s[...] = jnp.zeros((n_seq, IN_COLS), F32)

    @pl.when(s < W_STEPS)
    def _weight_chunk():
        w_in_bf = w_in_ref[0].astype(BF16)
        o_in_ref[...] = w_in_bf
        h_s[...] += _dot(xn_s[s], w_in_bf)
        for w_ref, o_ref, copy_s, axis in ((w_grp_ref, o_grp_ref, wgrp_s, 1), (w_brp_ref, o_brp_ref, wbrp_s, 0),
                                          (w_bra_ref, o_bra_ref, wbra_s, 0), (w_out_ref, o_out_ref, wout_s, 0)):
            w_bf = w_ref[0].astype(BF16)
            o_ref[...] = w_bf
            rows = w_bf.shape[axis]
            chunk_rows = pl.ds(pl.multiple_of(s * rows, rows), rows)
            if axis == 0:
                copy_s[chunk_rows, :] = w_bf
            else:
                copy_s[:, chunk_rows, :] = w_bf

    @pl.when(s == W_STEPS - 1)
    def _activations():
        slot = lax.broadcasted_iota(jnp.int32, (SUBLANES, WINDOW), 1)
        head = lax.broadcasted_iota(jnp.int32, (SUBLANES, WINDOW), 0)
        tbl = jnp.zeros((SUBLANES, WINDOW), F32)
        for h in range(N_Q_HEADS):
            tbl = jnp.where(head == h, _bias_table(WINDOW - slot, relb_ref, h), tbl)
        bias_s[...] = tbl
        place_s[...] = _placement_matrix()

        u_s[...] = h_s[:, OFF_U:OFF_U + POOL_WIDTH]
        zp_s[...] = _silu(h_s[:, OFF_ZP:OFF_ZP + POOL_WIDTH])
        q_heads = _head_blocks(h_s[:, OFF_Q:OFF_Q + ATTN_WIDTH] * (HEAD_DIM ** -0.5))
        for h in range(N_Q_HEADS):
            qh_s[:, h * LANES:(h + 1) * LANES] = q_heads[h]
        k_s[...] = h_s[:, OFF_K:OFF_K + KV_WIDTH]
        v_s[...] = h_s[:, OFF_V:OFF_V + KV_WIDTH]
        za_s[...] = _silu(h_s[:, OFF_ZA:OFF_ZA + ATTN_WIDTH])
        gp_s[...] = _sigmoid(h_s[:, OFF_GP:OFF_GP + D_MODEL])
        ga_s[...] = _sigmoid(h_s[:, OFF_GA:OFF_GA + D_MODEL])

    @pl.when(s >= W_STEPS)
    def _sequence_chunk():
        i = s - W_STEPS
        chunk = pl.ds(pl.multiple_of(i * BC, BC), BC)

        u_new = u_s[chunk, :]
        n_hist = POOL_STATE_ROWS
        parts = []
        for g, w in enumerate(POOL_WINDOWS):
            cols = slice(g * POOL_GROUP_DIM, (g + 1) * POOL_GROUP_DIM)
            acc = u_new[:, cols]
            for r in range(n_hist - (w - 1), n_hist):
                acc = acc + st_ref[r, :, cols]
            parts.append(acc * (1.0 / w) - u_new[:, cols])
        pooled_s[chunk, :] = jnp.concatenate(parts, axis=-1).astype(BF16)
        pnew_ref[0:n_hist - 1] = st_ref[1:n_hist]
        pnew_ref[n_hist - 1] = u_new

        k_new = k_s[chunk, :]
        v_new = v_s[chunk, :]
        last_slot = lax.broadcasted_iota(jnp.int32, (KV_WIDTH, WINDOW), 1) == WINDOW - 1
        k_cols = _last_slot_columns(k_new, place_s)
        v_cols = _last_slot_columns(v_new, place_s)
        for b in range(BC):
            lanes = slice(b * WINDOW, (b + 1) * WINDOW)
            knew_ref[b] = jnp.where(last_slot, k_cols[:, lanes], pltpu.roll(kt_ref[b], WINDOW - 1, 1))
            vnew_ref[b] = jnp.where(last_slot, v_cols[:, lanes], pltpu.roll(vt_ref[b], WINDOW - 1, 1))

        for h in range(N_Q_HEADS):
            q2_s[pl.ds(h, BC, stride=N_Q_HEADS), :] = qh_s[chunk, h * LANES:(h + 1) * LANES]
            kn2_s[pl.ds(h, BC, stride=N_Q_HEADS), :] = k_new
            vn2_s[pl.ds(h, BC, stride=N_Q_HEADS), :] = v_new
        q3 = q2_s[...].reshape(BC, N_Q_HEADS, LANES)
        kn3 = kn2_s[...].reshape(BC, N_Q_HEADS, LANES)
        vn3 = vn2_s[...].reshape(BC, N_Q_HEADS, LANES)
        s3 = lax.dot_general(q3.astype(BF16), kt_ref[...].astype(BF16), (((2,), (1,)), ((0,), (0,))),
                             preferred_element_type=F32) + bias_s[...][None]
        s_self = jnp.sum(q3 * kn3, axis=-1, keepdims=True) + _per_head_column(lambda h: relb_ref[h, 0])[None]
        sink = _per_head_column(lambda h: sinks_ref[h])[None]
        m = jnp.maximum(jnp.maximum(jnp.max(s3, axis=-1, keepdims=True), s_self), sink)
        p3 = jnp.exp(s3 - m)
        p_self = jnp.exp(s_self - m)
        denom = jnp.sum(p3, axis=-1, keepdims=True) + p_self + jnp.exp(sink - m)
        o3 = lax.dot_general(p3.astype(BF16), vt_ref[...].astype(BF16), (((2,), (2,)), ((0,), (0,))),
                             preferred_element_type=F32)
        o3 = (o3 + p_self * vn3) * (1.0 / denom)
        o2_s[...] = o3.reshape(BC * N_Q_HEADS, LANES)
        outs = [o2_s[pl.ds(h, BC, stride=N_Q_HEADS), :] for h in range(N_Q_HEADS)]
        att_s[chunk, :] = (_gather_heads(outs) * za_s[chunk, :]).astype(BF16)

    @pl.when(s == n_steps - 1)
    def _merge_all():
        merged = _merged_branches(pooled_s[...], zp_s[...], att_s[...], gp_s[...], ga_s[...],
                                  wgrp_s, pscale_ref[...], wbrp_s, wbra_s)
        out = x_ref[:, 0, :] + _dot(merged, wout_s[...])
        y_ref[:, 0, :] = _rmsnorm(out, gfinal_ref[...])


def _sample_call(x, cache_kt, cache_vt, state_t, relb, sinks, g_norm, pool_scale, g_final,
                 w_in, w_grp, w_brp, w_bra, w_out):
    n_seq = x.shape[0]
    assert n_seq % BC == 0

    def chunk_rows(n):
        assert n % (W_STEPS * 2 * SUBLANES) == 0
        return n // W_STEPS

    r_in, r_grp, r_br, r_out = (chunk_rows(D_MODEL), chunk_rows(POOL_GROUP_DIM), chunk_rows(POOL_WIDTH),
                                chunk_rows(D_MODEL))

    def w_step(s):
        return jnp.minimum(s, W_STEPS - 1)

    def seq_step(s):
        return jnp.maximum(s - W_STEPS, 0)

    weight_in_specs = [
        pl.BlockSpec((1, r_in, IN_COLS), lambda s: (0, w_step(s), 0)),
        pl.BlockSpec((1, POOL_GROUPS, r_grp, POOL_GROUP_DIM), lambda s: (0, 0, w_step(s), 0)),
        pl.BlockSpec((1, r_br, D_MODEL), lambda s: (0, w_step(s), 0)),
        pl.BlockSpec((1, r_br, D_MODEL), lambda s: (0, w_step(s), 0)),
        pl.BlockSpec((1, r_out, D_MODEL), lambda s: (0, w_step(s), 0)),
    ]
    weight_out_specs = (
        pl.BlockSpec((r_in, IN_COLS), lambda s: (w_step(s), 0)),
        pl.BlockSpec((POOL_GROUPS, r_grp, POOL_GROUP_DIM), lambda s: (0, w_step(s), 0)),
        pl.BlockSpec((r_br, D_MODEL), lambda s: (w_step(s), 0)),
        pl.BlockSpec((r_br, D_MODEL), lambda s: (w_step(s), 0)),
        pl.BlockSpec((r_out, D_MODEL), lambda s: (w_step(s), 0)),
    )
    out_shape = (
        jax.ShapeDtypeStruct((n_seq, 1, D_MODEL), F32),
        jax.ShapeDtypeStruct((n_seq, KV_WIDTH, WINDOW), F32),
        jax.ShapeDtypeStruct((n_seq, KV_WIDTH, WINDOW), F32),
        jax.ShapeDtypeStruct((POOL_STATE_ROWS, n_seq, POOL_WIDTH), F32),
        jax.ShapeDtypeStruct((D_MODEL, IN_COLS), BF16),
        jax.ShapeDtypeStruct((POOL_GROUPS, POOL_GROUP_DIM, POOL_GROUP_DIM), BF16),
        jax.ShapeDtypeStruct((POOL_WIDTH, D_MODEL), BF16),
        jax.ShapeDtypeStruct((ATTN_WIDTH, D_MODEL), BF16),
        jax.ShapeDtypeStruct((D_MODEL, D_MODEL), BF16),
    )
    scratch = [
        pltpu.VMEM((W_STEPS, n_seq, D_MODEL // W_STEPS), BF16),
        pltpu.VMEM((n_seq, IN_COLS), F32),
        pltpu.VMEM((n_seq, POOL_WIDTH), F32),
        pltpu.VMEM((n_seq, POOL_WIDTH), F32),
        pltpu.VMEM((n_seq, N_Q_HEADS * LANES), F32),
        pltpu.VMEM((n_seq, KV_WIDTH), F32),
        pltpu.VMEM((n_seq, KV_WIDTH), F32),
        pltpu.VMEM((n_seq, ATTN_WIDTH), F32),
        pltpu.VMEM((n_seq, D_MODEL), F32),
        pltpu.VMEM((n_seq, D_MODEL), F32),
        pltpu.VMEM((n_seq, POOL_WIDTH), BF16),
        pltpu.VMEM((n_seq, ATTN_WIDTH), BF16),
        pltpu.VMEM((BC * N_Q_HEADS, LANES), F32),
        pltpu.VMEM((BC * N_Q_HEADS, LANES), F32),
        pltpu.VMEM((BC * N_Q_HEADS, LANES), F32),
        pltpu.VMEM((BC * N_Q_HEADS, LANES), F32),
        pltpu.VMEM((SUBLANES, WINDOW), F32),
        pltpu.VMEM((3 * BC, BC * WINDOW), BF16),
        pltpu.VMEM((POOL_GROUPS, POOL_GROUP_DIM, POOL_GROUP_DIM), BF16),
        pltpu.VMEM((POOL_WIDTH, D_MODEL), BF16),
        pltpu.VMEM((ATTN_WIDTH, D_MODEL), BF16),
        pltpu.VMEM((D_MODEL, D_MODEL), BF16),
    ]
    cache = pl.BlockSpec((BC, KV_WIDTH, WINDOW), lambda s: (seq_step(s), 0, 0))
    hist = pl.BlockSpec((POOL_STATE_ROWS, BC, POOL_WIDTH), lambda s: (0, seq_step(s), 0))
    return pl.pallas_call(
        _sample_kernel,
        grid=(W_STEPS + n_seq // BC,),
        in_specs=[_SMEM, _SMEM, _resident((n_seq, 1, D_MODEL)), cache, cache, hist,
                  _resident((1, D_MODEL)), _resident((1, POOL_WIDTH)), _resident((1, D_MODEL))] + weight_in_specs,
        out_specs=(pl.BlockSpec((n_seq, 1, D_MODEL), lambda s: (0, 0, 0)), cache, cache, hist) + weight_out_specs,
        out_shape=out_shape,
        scratch_shapes=scratch,
        compiler_params=pltpu.CompilerParams(
            dimension_semantics=("arbitrary",), vmem_limit_bytes=VMEM_LIMIT_BYTES),
        name="sample_layer",
    )(relb, sinks, x, cache_kt, cache_vt, state_t, g_norm, pool_scale, g_final, w_in, w_grp, w_brp, w_bra, w_out)


def _cache_as_kd_slot(cache):
    n = cache.shape[0]
    return cache.transpose(0, 2, 3, 1).reshape(n, KV_WIDTH, WINDOW)


def _cache_from_kd_slot(cache_t):
    n = cache_t.shape[0]
    return cache_t.reshape(n, N_KV_HEADS, HEAD_DIM, WINDOW).transpose(0, 3, 1, 2)


def kernel(x_prompt, x_sample, cache_k, cache_v, state_pool, rel_bias, g_norm, w_in, pool_w_grp, pool_scale, attn_sinks, w_br_pool, w_br_attn, w_out, g_final):
    depth = g_norm.shape[0]
    assert depth == 1 and x_sample.shape[1] == 1
    l = 0
    sinks = attn_sinks[l]
    relb = rel_bias.T
    g_in, p_scale, g_out = g_norm[l].reshape(1, D_MODEL), pool_scale[l].reshape(1, POOL_WIDTH), g_final.reshape(1, D_MODEL)

    y_s, k_s, v_s, pool_s, w_in_bf, w_grp_bf, w_brp_bf, w_bra_bf, w_out_bf = _sample_call(
        x_sample, _cache_as_kd_slot(cache_k[l]), _cache_as_kd_slot(cache_v[l]),
        state_pool[l].transpose(1, 0, 2), relb, sinks, g_in, p_scale, g_out,
        w_in, pool_w_grp, w_br_pool, w_br_attn, w_out)
    weights = (g_in, w_in_bf, w_grp_bf, p_scale, w_brp_bf, w_bra_bf, w_out_bf, g_out)
    y_p, k_p, v_p, pool_p = _prompt_call(x_prompt, relb, sinks, weights)
    return (
        y_p,
        y_s,
        _cache_from_kd_slot(k_p)[None],
        _cache_from_kd_slot(v_p)[None],
        pool_p.transpose(1, 0, 2)[None],
        _cache_from_kd_slot(k_s)[None],
        _cache_from_kd_slot(v_s)[None],
        pool_s.transpose(1, 0, 2)[None],
    )
```
